```python
import math
import jax, jax.numpy as jnp
from jax import lax
import numpy as np

D_MODEL = 2048
BATCH = 8
SEQ = 2048
DEPTH = 2

HEAD_DIM = 128
N_HEADS = D_MODEL // HEAD_DIM
N_HEADS_A = (3 * N_HEADS) // 8
N_HEADS_B = (N_HEADS - N_HEADS_A) // 2
N_HEADS_C = N_HEADS - N_HEADS_A - N_HEADS_B
DILATION_PATTERNS = ((128, 1), (512, 4), (2048, 16))
BLOCK = 128
D_FF = ((8 * D_MODEL // 3 + 255) // 256) * 256
CONV_WIDTH = 3
EPS = 1e-6
NEG = -1e30
IN_SIZES = ([N_HEADS_A * HEAD_DIM] * 3 + [N_HEADS_B * HEAD_DIM] * 3
            + [N_HEADS_C * HEAD_DIM] * 3 + [N_HEADS_C])
N_IN = sum(IN_SIZES)

kernel_name = "hybrid_dilated_stickbreak_forgetting_convffn"


def rmsnorm(x, g):
    xf = x.astype(jnp.float32)
    y = xf * lax.rsqrt(jnp.mean(xf * xf, axis=-1, keepdims=True) + EPS)
    return (y * g.astype(jnp.float32)).astype(x.dtype)


def split_heads(t, h):
    b, s, _ = t.shape
    return t.reshape(b, s, h, HEAD_DIM).transpose(0, 2, 1, 3).astype(jnp.float32)


def alibi_slopes(n):
    return 2.0 ** (-8.0 * (jnp.arange(n, dtype=jnp.float32) + 1.0) / n)


def dilated_pattern(q, k, v, slopes, window, dil):
    b, h, t, d = q.shape
    length = t // dil
    nb = -(-length // BLOCK)
    lp = nb * BLOCK
    wsub = window // dil

    def to_sub(a):
        a = a.reshape(b, h, length, dil, d).transpose(0, 1, 3, 2, 4)
        a = jnp.pad(a, ((0, 0), (0, 0), (0, 0), (0, lp - length), (0, 0)))
        return a.reshape(b, h, dil, nb, BLOCK, d)

    def band(a):
        prev = jnp.concatenate([jnp.zeros_like(a[:, :, :, :1]), a[:, :, :, :-1]], axis=3)
        return jnp.concatenate([prev, a], axis=4)

    qb = to_sub(q)
    kw = band(to_sub(k))
    vw = band(to_sub(v))
    s = jnp.einsum('bhrnqd,bhrnkd->bhrnqk', qb, kw) / math.sqrt(d)
    qi = jnp.arange(BLOCK)[:, None]
    kc = jnp.arange(2 * BLOCK)[None, :]
    delta = BLOCK + qi - kc
    keypos = (jnp.arange(nb)[:, None, None] - 1) * BLOCK + kc[None]
    valid = (delta >= 0) & (delta <= wsub) & (keypos >= 0)
    alibi = -slopes[:, None, None, None, None] * (dil * delta).astype(jnp.float32)
    s = jnp.where(valid, s + alibi, NEG)
    lse = jax.nn.logsumexp(s, axis=-1)
    p = jnp.exp(s - lse[..., None])
    o = jnp.einsum('bhrnqk,bhrnkd->bhrnqd', p, vw)
    o = o.reshape(b, h, dil, lp, d)[:, :, :, :length].transpose(0, 1, 3, 2, 4).reshape(b, h, t, d)
    lse = lse.reshape(b, h, dil, lp)[:, :, :, :length].transpose(0, 1, 3, 2).reshape(b, h, t)
    return o, lse


def dilated_mixture(q, k, v):
    slopes = alibi_slopes(q.shape[1])
    outs, lses = [], []
    for window, dil in DILATION_PATTERNS:
        o, lse = dilated_pattern(q, k, v, slopes, window, dil)
        outs.append(o)
        lses.append(lse)
    w = jax.nn.softmax(jnp.stack(lses, axis=0), axis=0)
    return jnp.sum(w[..., None] * jnp.stack(outs, axis=0), axis=0)


def stick_breaking(q, k, v):
    b, h, t, d = q.shape
    nb = t // BLOCK
    qb = q.reshape(b, h, nb, BLOCK, d).transpose(2, 0, 1, 3, 4)
    kpos = jnp.arange(t)
    scale = 1.0 / math.sqrt(d)

    def one(args):
        qblk, n = args
        z = jnp.einsum('bhqd,bhkd->bhqk', qblk, k) * scale
        qpos = n * BLOCK + jnp.arange(BLOCK)
        causal = kpos[None, :] < qpos[:, None]
        log_1m = jnp.where(causal, jax.nn.log_sigmoid(-z), 0.0)
        after = lax.cumsum(log_1m, axis=3, reverse=True) - log_1m
        a = jnp.where(causal, jnp.exp(jax.nn.log_sigmoid(z) + after), 0.0)
        return jnp.einsum('bhqk,bhkd->bhqd', a, v)

    o = lax.map(one, (qb, jnp.arange(nb)))
    return o.transpose(1, 2, 0, 3, 4).reshape(b, h, t, d)


def forgetting_attention(q, k, v, log_f):
    b, h, t, d = q.shape
    nb = t // BLOCK
    c = lax.cumsum(log_f, axis=2)
    qb = q.reshape(b, h, nb, BLOCK, d).transpose(2, 0, 1, 3, 4)
    cb = c.reshape(b, h, nb, BLOCK).transpose(2, 0, 1, 3)
    kpos = jnp.arange(t)
    scale = 1.0 / math.sqrt(d)

    def one(args):
        qblk, cq, n = args
        s = jnp.einsum('bhqd,bhkd->bhqk', qblk, k) * scale + cq[..., None] - c[:, :, None, :]
        qpos = n * BLOCK + jnp.arange(BLOCK)
        causal = kpos[None, :] <= qpos[:, None]
        p = jax.nn.softmax(jnp.where(causal, s, NEG), axis=-1)
        return jnp.einsum('bhqk,bhkd->bhqd', p, v)

    o = lax.map(one, (qb, cb, jnp.arange(nb)))
    return o.transpose(1, 2, 0, 3, 4).reshape(b, h, t, d)


def causal_dwconv(hdn, w, bias):
    t = hdn.shape[1]
    padded = jnp.pad(hdn, ((0, 0), (CONV_WIDTH - 1, 0), (0, 0)))
    out = bias
    for j in range(CONV_WIDTH):
        out = out + w[j] * padded[:, j:j + t]
    return out


def hybrid_layer(x, g_mix, w_in, b_f, g_head, w_o, g_ffn, w_gu, conv_w, conv_b, w_down):
    b, t, _ = x.shape
    hn = rmsnorm(x, g_mix)
    proj = hn @ w_in
    idx = [int(v) for v in np.cumsum(IN_SIZES)[:-1]]
    qa, ka, va, qb_, kb, vb, qc, kc, vc, fc = jnp.split(proj, idx, axis=-1)
    o_a = dilated_mixture(split_heads(qa, N_HEADS_A), split_heads(ka, N_HEADS_A),
                          split_heads(va, N_HEADS_A))
    o_b = stick_breaking(split_heads(qb_, N_HEADS_B), split_heads(kb, N_HEADS_B),
                         split_heads(vb, N_HEADS_B))
    log_f = jax.nn.log_sigmoid(fc.astype(jnp.float32) + b_f.astype(jnp.float32))
    o_c = forgetting_attention(split_heads(qc, N_HEADS_C), split_heads(kc, N_HEADS_C),
                               split_heads(vc, N_HEADS_C), log_f.transpose(0, 2, 1))
    o = jnp.concatenate([o_a, o_b, o_c], axis=1)
    o = o * lax.rsqrt(jnp.mean(o * o, axis=-1, keepdims=True) + EPS) \
        * g_head.astype(jnp.float32)[None, :, None, :]
    o = o.transpose(0, 2, 1, 3).reshape(b, t, D_MODEL).astype(x.dtype)
    x = x + o @ w_o
    hn = rmsnorm(x, g_ffn)
    gu = causal_dwconv(hn @ w_gu, conv_w, conv_b)
    gate, up = jnp.split(gu, 2, axis=-1)
    return x + (jax.nn.silu(gate) * up) @ w_down


def setup_inputs(seed: int = 0) -> dict:
    key = jax.random.key(seed)
    ks = jax.random.split(key, 12)
    f32 = jnp.float32
    nrm = jax.random.normal
    return {
        "x": nrm(ks[0], (BATCH, SEQ, D_MODEL), f32),
        "g_mix": 1.0 + 0.02 * nrm(ks[1], (DEPTH, D_MODEL), f32),
        "w_in": nrm(ks[2], (DEPTH, D_MODEL, N_IN), f32) * D_MODEL ** -0.5,
        "b_f": 1.0 + 3.0 * jax.random.uniform(ks[3], (DEPTH, N_HEADS_C), f32),
        "g_head": 1.0 + 0.02 * nrm(ks[4], (DEPTH, N_HEADS, HEAD_DIM), f32),
        "w_o": nrm(ks[5], (DEPTH, D_MODEL, D_MODEL), f32) * D_MODEL ** -0.5,
        "g_ffn": 1.0 + 0.02 * nrm(ks[6], (DEPTH, D_MODEL), f32),
        "w_gu": nrm(ks[7], (DEPTH, D_MODEL, 2 * D_FF), f32) * D_MODEL ** -0.5,
        "conv_w": nrm(ks[8], (DEPTH, CONV_WIDTH, 2 * D_FF), f32) * CONV_WIDTH ** -0.5,
        "conv_b": 0.02 * nrm(ks[9], (DEPTH, 2 * D_FF), f32),
        "w_down": nrm(ks[10], (DEPTH, D_FF, D_MODEL), f32) * D_FF ** -0.5,
        "g_final": 1.0 + 0.02 * nrm(ks[11], (D_MODEL,), f32),
    }


def reference(x, g_mix, w_in, b_f, g_head, w_o, g_ffn, w_gu, conv_w, conv_b, w_down, g_final):
    for layer in range(DEPTH):
        x = hybrid_layer(x, g_mix[layer], w_in[layer], b_f[layer], g_head[layer], w_o[layer],
                         g_ffn[layer], w_gu[layer], conv_w[layer], conv_b[layer], w_down[layer])
    return rmsnorm(x, g_final)
```

```python
import functools
import math

import jax
import jax.numpy as jnp
from jax import lax
from jax.experimental import pallas as pl
from jax.experimental.pallas import tpu as pltpu

F32 = jnp.float32
BF16 = jnp.bfloat16

HEAD_DIM = 128
N_HEADS_A = 6
N_HEADS_B = 5
N_HEADS_C = 5
N_HEADS = N_HEADS_A + N_HEADS_B + N_HEADS_C
DILATION_PATTERNS = ((128, 1), (512, 4), (2048, 16))
WINDOW_BLOCK = 128
CONV_WIDTH = 3
EPS = 1e-6
NEG = -1e30
LANES = 128
SUBLANES = 8
VMEM_LIMIT_BYTES = 56 * 1024 * 1024

QA0, KA0, VA0 = 0, N_HEADS_A, 2 * N_HEADS_A
QB0 = 3 * N_HEADS_A
KB0, VB0 = QB0 + N_HEADS_B, QB0 + 2 * N_HEADS_B
QC0 = QB0 + 3 * N_HEADS_B
KC0, VC0 = QC0 + N_HEADS_C, QC0 + 2 * N_HEADS_C
N_QKV_BLOCKS = QC0 + 3 * N_HEADS_C

ATT_BQ = 256
ATT_CHUNK = 256
FOX_CHUNK = 512


def _params(*sem):
    return pltpu.CompilerParams(dimension_semantics=sem, vmem_limit_bytes=VMEM_LIMIT_BYTES)


def _nt_dot(a, b):
    return lax.dot_general(a, b, (((1,), (1,)), ((), ())), preferred_element_type=F32)


def _head_rmsnorm(o, g):
    return o * lax.rsqrt(jnp.mean(o * o, axis=-1, keepdims=True) + EPS) * g


def _log_sigmoid(x):
    return jnp.minimum(x, 0.0) - jnp.log(1.0 + jnp.exp(-jnp.abs(x)))


def _inproj_kernel(x_ref, g_ref, w_ref, wf_ref, qkv_ref, f_ref, hn_ref):
    @pl.when(pl.program_id(1) == 0)
    def _():
        x = x_ref[...]
        y = x * lax.rsqrt(jnp.mean(x * x, axis=-1, keepdims=True) + EPS)
        hn = (y * g_ref[...]).astype(BF16)
        hn_ref[...] = hn
        f_ref[...] = jnp.dot(hn, wf_ref[...], preferred_element_type=F32)

    acc = jnp.dot(hn_ref[...], w_ref[...], preferred_element_type=F32)
    for c in range(acc.shape[1] // LANES):
        qkv_ref[c] = acc[:, c * LANES:(c + 1) * LANES].astype(BF16)


def _inproj(x, g, w, wf, bm=1024, bn=1024):
    m, d = x.shape
    n = w.shape[1]
    return pl.pallas_call(
        _inproj_kernel,
        grid=(m // bm, n // bn),
        in_specs=[
            pl.BlockSpec((bm, d), lambda i, j: (i, 0)),
            pl.BlockSpec((1, d), lambda i, j: (0, 0)),
            pl.BlockSpec((d, bn), lambda i, j: (0, j)),
            pl.BlockSpec((d, LANES), lambda i, j: (0, 0)),
        ],
        out_specs=[
            pl.BlockSpec((bn // LANES, bm, LANES), lambda i, j: (j, i, 0)),
            pl.BlockSpec((bm, LANES), lambda i, j: (i, 0)),
        ],
        out_shape=[
            jax.ShapeDtypeStruct((n // LANES, m, LANES), BF16),
            jax.ShapeDtypeStruct((m, LANES), F32),
        ],
        scratch_shapes=[pltpu.VMEM((bm, d), BF16)],
        compiler_params=_params("arbitrary", "arbitrary"),
        name="inproj",
    )(x, g, w, wf)


def _gate_cumsum_kernel(f_ref, b_ref, ccol_ref, crow_ref):
    t = f_ref.shape[0]
    blk = LANES
    r_i = lax.broadcasted_iota(jnp.int32, (blk, blk), 0)
    c_i = lax.broadcasted_iota(jnp.int32, (blk, blk), 1)
    tri = jnp.where(c_i <= r_i, 1.0, 0.0).astype(BF16)
    carry = jnp.zeros((1, LANES), F32)
    for n in range(t // blk):
        lf = _log_sigmoid(f_ref[n * blk:(n + 1) * blk, :] + b_ref[...])
        h1 = lf.astype(BF16)
        r1 = lf - h1.astype(F32)
        h2 = r1.astype(BF16)
        h3 = (r1 - h2.astype(F32)).astype(BF16)
        c = (jnp.dot(tri, h1, preferred_element_type=F32)
             + jnp.dot(tri, h2, preferred_element_type=F32)
             + jnp.dot(tri, h3, preferred_element_type=F32)) + carry
        ccol_ref[n * blk:(n + 1) * blk, :] = c
        crow_ref[:, n * blk:(n + 1) * blk] = jnp.transpose(c)[0:SUBLANES, :]
        carry = c[blk - 1:blk, :]


def _gate_cumsum(f, b_pad, t):
    m = f.shape[0]
    nb = m // t
    return pl.pallas_call(
        _gate_cumsum_kernel,
        grid=(nb,),
        in_specs=[
            pl.BlockSpec((t, LANES), lambda b: (b, 0)),
            pl.BlockSpec((1, LANES), lambda b: (0, 0)),
        ],
        out_specs=[
            pl.BlockSpec((t, LANES), lambda b: (b, 0)),
            pl.BlockSpec((None, SUBLANES, t), lambda b: (b, 0, 0)),
        ],
        out_shape=[
            jax.ShapeDtypeStruct((m, LANES), F32),
            jax.ShapeDtypeStruct((nb, SUBLANES, t), F32),
        ],
        compiler_params=_params("arbitrary"),
        name="gate_cumsum",
    )(f, b_pad)


def _dilated_kernel(slopes_ref, q_ref, k_ref, v_ref, g_ref, o_ref,
                    qf_ref, kf_ref, vf_ref, oacc_ref, lse_ref):
    h = pl.program_id(1)
    t = q_ref.shape[0]
    wb = WINDOW_BLOCK
    slope = slopes_ref[h]
    qf_ref[...] = q_ref[...].astype(F32)
    kf_ref[...] = k_ref[...].astype(F32)
    vf_ref[...] = v_ref[...].astype(F32)

    q_i = lax.broadcasted_iota(jnp.int32, (wb, wb), 0)
    k_i = lax.broadcasted_iota(jnp.int32, (wb, wb), 1)
    delta = (q_i - k_i).astype(F32)

    for p, (window, dil) in enumerate(DILATION_PATTERNS):
        assert window // dil == wb and t % (dil * wb) == 0
        bias_cur = jnp.where(k_i <= q_i, (-slope * dil) * delta, NEG)
        bias_prev = jnp.where(k_i >= q_i, (-slope * dil) * (delta + wb), NEG)
        for r in range(dil):
            k_prev = v_prev = None
            for n in range(t // (dil * wb)):
                start = r + dil * wb * n
                rows = pl.ds(start, wb, stride=dil) if dil > 1 else pl.ds(start, wb)
                qb = qf_ref[rows, :].astype(BF16)
                k_cur = kf_ref[rows, :].astype(BF16)
                v_cur = vf_ref[rows, :].astype(BF16)
                s_c = _nt_dot(qb, k_cur) + bias_cur
                mx = jnp.max(s_c, axis=-1, keepdims=True)
                if n > 0:
                    s_p = _nt_dot(qb, k_prev) + bias_prev
                    mx = jnp.maximum(mx, jnp.max(s_p, axis=-1, keepdims=True))
                p_c = jnp.exp(s_c - mx)
                l = jnp.sum(p_c, axis=-1, keepdims=True)
                o = jnp.dot(p_c.astype(BF16), v_cur, preferred_element_type=F32)
                if n > 0:
                    p_p = jnp.exp(s_p - mx)
                    l = l + jnp.sum(p_p, axis=-1, keepdims=True)
                    o = o + jnp.dot(p_p.astype(BF16), v_prev, preferred_element_type=F32)
                oacc_ref[p, rows, :] = o / l
                lse_ref[p, rows, :] = jnp.broadcast_to(mx + jnp.log(l), (wb, LANES))
                k_prev, v_prev = k_cur, v_cur

    g = g_ref[pl.ds(h, 1), :]
    rb = 256
    for c in range(t // rb):
        rs = slice(c * rb, (c + 1) * rb)
        l0, l1, l2 = lse_ref[0, rs, :], lse_ref[1, rs, :], lse_ref[2, rs, :]
        mx = jnp.maximum(jnp.maximum(l0, l1), l2)
        e0, e1, e2 = jnp.exp(l0 - mx), jnp.exp(l1 - mx), jnp.exp(l2 - mx)
        o = (e0 * oacc_ref[0, rs, :] + e1 * oacc_ref[1, rs, :] + e2 * oacc_ref[2, rs, :]) / (e0 + e1 + e2)
        o_ref[rs, :] = _head_rmsnorm(o, g).astype(BF16)


def _dilated_attention(qkv, g_head, slopes, nb, t):
    m = nb * t
    npat = len(DILATION_PATTERNS)
    grid_spec = pltpu.PrefetchScalarGridSpec(
        num_scalar_prefetch=1,
        grid=(nb, N_HEADS_A),
        in_specs=[
            pl.BlockSpec((None, t, HEAD_DIM), lambda b, h, s: (QA0 + h, b, 0)),
            pl.BlockSpec((None, t, HEAD_DIM), lambda b, h, s: (KA0 + h, b, 0)),
            pl.BlockSpec((None, t, HEAD_DIM), lambda b, h, s: (VA0 + h, b, 0)),
            pl.BlockSpec((N_HEADS, HEAD_DIM), lambda b, h, s: (0, 0)),
        ],
        out_specs=pl.BlockSpec((t, HEAD_DIM), lambda b, h, s: (b, h)),
        scratch_shapes=[
            pltpu.VMEM((t, HEAD_DIM), F32),
            pltpu.VMEM((t, HEAD_DIM), F32),
            pltpu.VMEM((t, HEAD_DIM), F32),
            pltpu.VMEM((npat, t, HEAD_DIM), F32),
            pltpu.VMEM((npat, t, LANES), F32),
        ],
    )
    return pl.pallas_call(
        _dilated_kernel,
        grid_spec=grid_spec,
        out_shape=jax.ShapeDtypeStruct((m, N_HEADS_A * HEAD_DIM), BF16),
        compiler_params=_params("arbitrary", "arbitrary"),
        name="dilated_attention",
    )(slopes, qkv, qkv, qkv, g_head)


def _stickbreak_kernel(q_ref, k_ref, v_ref, g_ref, o_ref, a_ref):
    j = pl.program_id(1)
    t = q_ref.shape[0]
    bq, ck = ATT_BQ, ATT_CHUNK
    assert bq == ck
    g = g_ref[pl.ds(N_HEADS_A + j, 1), :]
    r_i = lax.broadcasted_iota(jnp.int32, (ck, ck), 0)
    c_i = lax.broadcasted_iota(jnp.int32, (ck, ck), 1)
    suffix = jnp.where(r_i > c_i, 1.0, 0.0).astype(BF16)
    suffix2 = jnp.concatenate([suffix, suffix], axis=0)
    causal = c_i < r_i
    for n in range(t // bq):
        q = q_ref[n * bq:(n + 1) * bq, :]
        later = jnp.zeros((bq, 1), F32)
        for c in range(n, -1, -1):
            z = _nt_dot(q, k_ref[c * ck:(c + 1) * ck, :])
            log_1m = jnp.minimum(-z, 0.0) - jnp.log(1.0 + jnp.exp(-jnp.abs(z)))
            log_sig = z + log_1m
            if c == n:
                log_1m = jnp.where(causal, log_1m, 0.0)
            hi = log_1m.astype(BF16)
            lo = (log_1m - hi.astype(F32)).astype(BF16)
            after = jnp.dot(jnp.concatenate([hi, lo], axis=1), suffix2,
                            preferred_element_type=F32) + later
            a = jnp.exp(log_sig + after)
            if c == n:
                a = jnp.where(causal, a, 0.0)
            a_ref[:, c * ck:(c + 1) * ck] = a.astype(BF16)
            if c > 0:
                later = later + jnp.sum(log_1m, axis=-1, keepdims=True)
        ext = (n + 1) * bq
        o = jnp.dot(a_ref[:, 0:ext], v_ref[0:ext, :], preferred_element_type=F32)
        o_ref[n * bq:(n + 1) * bq, :] = _head_rmsnorm(o, g).astype(BF16)


def _stickbreak_attention(qkv, g_head, nb, t):
    m = nb * t
    return pl.pallas_call(
        _stickbreak_kernel,
        grid=(nb, N_HEADS_B),
        in_specs=[
            pl.BlockSpec((None, t, HEAD_DIM), lambda b, j: (QB0 + j, b, 0)),
            pl.BlockSpec((None, t, HEAD_DIM), lambda b, j: (KB0 + j, b, 0)),
            pl.BlockSpec((None, t, HEAD_DIM), lambda b, j: (VB0 + j, b, 0)),
            pl.BlockSpec((N_HEADS, HEAD_DIM), lambda b, j: (0, 0)),
        ],
        out_specs=pl.BlockSpec((t, HEAD_DIM), lambda b, j: (b, j)),
        out_shape=jax.ShapeDtypeStruct((m, N_HEADS_B * HEAD_DIM), BF16),
        scratch_shapes=[pltpu.VMEM((ATT_BQ, t), BF16)],
        compiler_params=_params("arbitrary", "arbitrary"),
        name="stickbreak_attention",
    )(qkv, qkv, qkv, g_head)


def _forgetting_kernel(q_ref, k_ref, v_ref, ccol_ref, crow_ref, g_ref, o_ref, s_ref, p_ref):
    j = pl.program_id(1)
    t = q_ref.shape[0]
    bq, ck = ATT_BQ, FOX_CHUNK
    g = g_ref[pl.ds(N_HEADS_A + N_HEADS_B + j, 1), :]
    lane = lax.broadcasted_iota(jnp.int32, (1, LANES), 1)
    r_i = lax.broadcasted_iota(jnp.int32, (bq, bq), 0)
    c_i = lax.broadcasted_iota(jnp.int32, (bq, bq), 1)
    causal = c_i <= r_i
    for n in range(t // bq):
        q0, ext = n * bq, (n + 1) * bq
        q = q_ref[q0:ext, :]
        c_t = jnp.sum(jnp.where(lane == j, ccol_ref[q0:ext, :], 0.0), axis=-1, keepdims=True)
        mx = None
        for k0 in range(0, ext, ck):
            k1 = min(k0 + ck, ext)
            s = _nt_dot(q, k_ref[k0:k1, :]) + c_t - crow_ref[pl.ds(j, 1), k0:k1]
            if k1 == ext:
                split = k1 - k0 - bq
                diag = jnp.where(causal, s[:, split:], NEG)
                s = diag if split == 0 else jnp.concatenate([s[:, :split], diag], axis=1)
            s_ref[:, k0:k1] = s
            cm = jnp.max(s, axis=-1, keepdims=True)
            mx = cm if mx is None else jnp.maximum(mx, cm)
        l = jnp.zeros((bq, 1), F32)
        for k0 in range(0, ext, ck):
            k1 = min(k0 + ck, ext)
            p = jnp.exp(s_ref[:, k0:k1] - mx)
            l = l + jnp.sum(p, axis=-1, keepdims=True)
            p_ref[:, k0:k1] = p.astype(BF16)
        o = jnp.dot(p_ref[:, 0:ext], v_ref[0:ext, :], preferred_element_type=F32) / l
        o_ref[q0:ext, :] = _head_rmsnorm(o, g).astype(BF16)


def _forgetting_attention(qkv, ccol, crow, g_head, nb, t):
    m = nb * t
    return pl.pallas_call(
        _forgetting_kernel,
        grid=(nb, N_HEADS_C),
        in_specs=[
            pl.BlockSpec((None, t, HEAD_DIM), lambda b, j: (QC0 + j, b, 0)),
            pl.BlockSpec((None, t, HEAD_DIM), lambda b, j: (KC0 + j, b, 0)),
            pl.BlockSpec((None, t, HEAD_DIM), lambda b, j: (VC0 + j, b, 0)),
            pl.BlockSpec((t, LANES), lambda b, j: (b, 0)),
            pl.BlockSpec((None, SUBLANES, t), lambda b, j: (b, 0, 0)),
            pl.BlockSpec((N_HEADS, HEAD_DIM), lambda b, j: (0, 0)),
        ],
        out_specs=pl.BlockSpec((t, HEAD_DIM), lambda b, j: (b, j)),
        out_shape=jax.ShapeDtypeStruct((m, N_HEADS_C * HEAD_DIM), BF16),
        scratch_shapes=[pltpu.VMEM((ATT_BQ, t), F32), pltpu.VMEM((ATT_BQ, t), BF16)],
        compiler_params=_params("arbitrary", "arbitrary"),
        name="forgetting_attention",
    )(qkv, qkv, qkv, ccol, crow, g_head)


def _outproj_kernel(x_ref, oa_ref, ob_ref, oc_ref, wa_ref, wb_ref, wc_ref, g_ref, x1_ref, hn_ref):
    x1 = (x_ref[...]
          + jnp.dot(oa_ref[...], wa_ref[...], preferred_element_type=F32)
          + jnp.dot(ob_ref[...], wb_ref[...], preferred_element_type=F32)
          + jnp.dot(oc_ref[...], wc_ref[...], preferred_element_type=F32))
    x1_ref[...] = x1
    y = x1 * lax.rsqrt(jnp.mean(x1 * x1, axis=-1, keepdims=True) + EPS)
    hn_ref[...] = (y * g_ref[...]).astype(BF16)


def _outproj(x, oa, ob, oc, wa, wb, wc, g, bm=512):
    m, d = x.shape
    row = lambda i: (i, 0)
    whole = lambda i: (0, 0)
    return pl.pallas_call(
        _outproj_kernel,
        grid=(m // bm,),
        in_specs=[
            pl.BlockSpec((bm, d), row),
            pl.BlockSpec((bm, oa.shape[1]), row),
            pl.BlockSpec((bm, ob.shape[1]), row),
            pl.BlockSpec((bm, oc.shape[1]), row),
            pl.BlockSpec(wa.shape, whole),
            pl.BlockSpec(wb.shape, whole),
            pl.BlockSpec(wc.shape, whole),
            pl.BlockSpec((1, d), whole),
        ],
        out_specs=[pl.BlockSpec((bm, d), row), pl.BlockSpec((bm, d), row)],
        out_shape=[jax.ShapeDtypeStruct((m, d), F32), jax.ShapeDtypeStruct((m, d), BF16)],
        compiler_params=_params("arbitrary"),
        name="outproj",
    )(x, oa, ob, oc, wa, wb, wc, g)


def _ffn_up_kernel(hn_ref, wg_ref, wu_ref, cwg_ref, cwu_ref, cbg_ref, cbu_ref, h_ref,
                   halo_g_ref, halo_u_ref, *, tiles_per_seq):
    i, j = pl.program_id(0), pl.program_id(1)
    bm = hn_ref.shape[0]
    hn = hn_ref[...]
    gate = jnp.dot(hn, wg_ref[...], preferred_element_type=F32)
    up = jnp.dot(hn, wu_ref[...], preferred_element_type=F32)

    @pl.when(i % tiles_per_seq == 0)
    def _():
        halo_g_ref[j] = jnp.zeros(halo_g_ref.shape[1:], F32)
        halo_u_ref[j] = jnp.zeros(halo_u_ref.shape[1:], F32)

    def conv(a, prev, cw, cb):
        ext = jnp.concatenate([prev, a], axis=0)
        a1 = ext[SUBLANES - 1:SUBLANES - 1 + bm, :]
        a2 = ext[SUBLANES - 2:SUBLANES - 2 + bm, :]
        return cb + cw[0:1, :] * a2 + cw[1:2, :] * a1 + cw[2:3, :] * a

    gate_c = conv(gate, halo_g_ref[j], cwg_ref[...], cbg_ref[...])
    up_c = conv(up, halo_u_ref[j], cwu_ref[...], cbu_ref[...])
    halo_g_ref[j] = gate[bm - SUBLANES:, :]
    halo_u_ref[j] = up[bm - SUBLANES:, :]
    h_ref[...] = (gate_c * (1.0 / (1.0 + jnp.exp(-gate_c))) * up_c).astype(BF16)


def _ffn_up(hn, w_gu, conv_w, conv_b, t, bm=1024, bf=512):
    m, d = hn.shape
    dff = w_gu.shape[1] // 2
    nj = dff // bf
    assert t % bm == 0
    return pl.pallas_call(
        functools.partial(_ffn_up_kernel, tiles_per_seq=t // bm),
        grid=(m // bm, nj),
        in_specs=[
            pl.BlockSpec((bm, d), lambda i, j: (i, 0)),
            pl.BlockSpec((d, bf), lambda i, j: (0, j)),
            pl.BlockSpec((d, bf), lambda i, j: (0, nj + j)),
            pl.BlockSpec((CONV_WIDTH, bf), lambda i, j: (0, j)),
            pl.BlockSpec((CONV_WIDTH, bf), lambda i, j: (0, nj + j)),
            pl.BlockSpec((1, bf), lambda i, j: (0, j)),
            pl.BlockSpec((1, bf), lambda i, j: (0, nj + j)),
        ],
        out_specs=pl.BlockSpec((bm, bf), lambda i, j: (i, j)),
        out_shape=jax.ShapeDtypeStruct((m, dff), BF16),
        scratch_shapes=[pltpu.VMEM((nj, SUBLANES, bf), F32), pltpu.VMEM((nj, SUBLANES, bf), F32)],
        compiler_params=_params("arbitrary", "arbitrary"),
        name="ffn_up",
    )(hn, w_gu, w_gu, conv_w, conv_w, conv_b, conv_b)


def _ffn_down_kernel(x_ref, h_ref, w_ref, o_ref):
    o_ref[...] = x_ref[...] + jnp.dot(h_ref[...], w_ref[...], preferred_element_type=F32)


def _ffn_down(x, h, w, bm=1024, bn=512):
    m, d = x.shape
    k = h.shape[1]
    return pl.pallas_call(
        _ffn_down_kernel,
        grid=(m // bm, d // bn),
        in_specs=[
            pl.BlockSpec((bm, bn), lambda i, j: (i, j)),
            pl.BlockSpec((bm, k), lambda i, j: (i, 0)),
            pl.BlockSpec((k, bn), lambda i, j: (0, j)),
        ],
        out_specs=pl.BlockSpec((bm, bn), lambda i, j: (i, j)),
        out_shape=jax.ShapeDtypeStruct((m, d), F32),
        compiler_params=_params("arbitrary", "arbitrary"),
        name="ffn_down",
    )(x, h, w)


def _rmsnorm_kernel(x_ref, g_ref, o_ref):
    x = x_ref[...]
    y = x * lax.rsqrt(jnp.mean(x * x, axis=-1, keepdims=True) + EPS)
    o_ref[...] = y * g_ref[...]


def _rmsnorm(x, g, bm=1024):
    m, d = x.shape
    return pl.pallas_call(
        _rmsnorm_kernel,
        grid=(m // bm,),
        in_specs=[pl.BlockSpec((bm, d), lambda i: (i, 0)), pl.BlockSpec((1, d), lambda i: (0, 0))],
        out_specs=pl.BlockSpec((bm, d), lambda i: (i, 0)),
        out_shape=jax.ShapeDtypeStruct((m, d), F32),
        compiler_params=_params("arbitrary"),
        name="final_rmsnorm",
    )(x, g)


def _layer(x, nb, t, g_mix, w_in, b_f, g_head, w_o, g_ffn, w_gu, conv_w, conv_b, w_down, slopes):
    d = x.shape[1]
    n_qkv = N_QKV_BLOCKS * HEAD_DIM
    scale = 1.0 / math.sqrt(HEAD_DIM)
    blk = jnp.arange(N_QKV_BLOCKS)
    is_q = ((blk < KA0) | ((blk >= QB0) & (blk < KB0)) | ((blk >= QC0) & (blk < KC0)))
    col_scale = jnp.repeat(jnp.where(is_q, scale, 1.0).astype(F32), HEAD_DIM)
    w_qkv = (w_in[:, :n_qkv] * col_scale[None, :]).astype(BF16)
    w_f = jnp.pad(w_in[:, n_qkv:], ((0, 0), (0, LANES - N_HEADS_C))).astype(BF16)
    b_pad = jnp.pad(b_f, (0, LANES - N_HEADS_C)).reshape(1, LANES)

    qkv, f = _inproj(x, g_mix.reshape(1, d), w_qkv, w_f)
    ccol, crow = _gate_cumsum(f, b_pad, t)
    o_a = _dilated_attention(qkv, g_head, slopes, nb, t)
    o_b = _stickbreak_attention(qkv, g_head, nb, t)
    o_c = _forgetting_attention(qkv, ccol, crow, g_head, nb, t)

    w_o16 = w_o.astype(BF16)
    ka, kb = N_HEADS_A * HEAD_DIM, (N_HEADS_A + N_HEADS_B) * HEAD_DIM
    x1, hn = _outproj(x, o_a, o_b, o_c, w_o16[:ka], w_o16[ka:kb], w_o16[kb:], g_ffn.reshape(1, d))
    h = _ffn_up(hn, w_gu.astype(BF16), conv_w, conv_b.reshape(1, -1), t)
    return _ffn_down(x1, h, w_down.astype(BF16))


def kernel(x, g_mix, w_in, b_f, g_head, w_o, g_ffn, w_gu, conv_w, conv_b, w_down, g_final):
    nb, t, d = x.shape
    depth = g_mix.shape[0]
    slopes = 2.0 ** (-8.0 * (jnp.arange(N_HEADS_A, dtype=F32) + 1.0) / N_HEADS_A)
    xf = x.reshape(nb * t, d)
    for layer in range(depth):
        xf = _layer(xf, nb, t, g_mix[layer], w_in[layer], b_f[layer], g_head[layer], w_o[layer],
                    g_ffn[layer], w_gu[layer], conv_w[layer], conv_b[layer], w_down[layer], slopes)
    return _rmsnorm(xf, g_final.reshape(1, d)).reshape(nb, t, d)
```

```python
import functools
import math

import jax
import jax.numpy as jnp
from jax import lax
from jax.experimental import pallas as pl
from jax.experimental.pallas import tpu as pltpu

F32 = jnp.float32
BF16 = jnp.bfloat16

HEAD_DIM = 128
N_HEADS_A = 6
N_HEADS_B = 5
N_HEADS_C = 5
N_HEADS = N_HEADS_A + N_HEADS_B + N_HEADS_C
DILATION_PATTERNS = ((128, 1), (512, 4), (2048, 16))
WINDOW_BLOCK = 128
CONV_WIDTH = 3
EPS = 1e-6
NEG = -1e30
LANES = 128
SUBLANES = 8
VMEM_LIMIT_BYTES = 56 * 1024 * 1024
LHS_PAD = LANES

QA0, KA0, VA0 = 0, N_HEADS_A, 2 * N_HEADS_A
QB0 = 3 * N_HEADS_A
KB0, VB0 = QB0 + N_HEADS_B, QB0 + 2 * N_HEADS_B
QC0 = QB0 + 3 * N_HEADS_B
KC0, VC0 = QC0 + N_HEADS_C, QC0 + 2 * N_HEADS_C
N_QKV_BLOCKS = QC0 + 3 * N_HEADS_C

ATT_BQ = 256
ATT_CHUNK = 256
FOX_CHUNK = 512


def _params(*sem):
    return pltpu.CompilerParams(dimension_semantics=sem, vmem_limit_bytes=VMEM_LIMIT_BYTES)


def _nt_dot(a, b):
    return lax.dot_general(a, b, (((1,), (1,)), ((), ())), preferred_element_type=F32)


def _head_rmsnorm(o, g):
    return o * lax.rsqrt(jnp.mean(o * o, axis=-1, keepdims=True) + EPS) * g


def _log_sigmoid(x):
    return jnp.minimum(x, 0.0) - jnp.log(1.0 + jnp.exp(-jnp.abs(x)))


def _inproj_kernel(x_ref, g_ref, w_ref, wf_ref, qkv_ref, f_ref, hn_ref):
    d = x_ref.shape[1]

    @pl.when(pl.program_id(1) == 0)
    def _():
        x = x_ref[...]
        y = x * lax.rsqrt(jnp.mean(x * x, axis=-1, keepdims=True) + EPS)
        hn_ref[:, 0:d] = (y * g_ref[...]).astype(BF16)
        f_ref[...] = jnp.dot(hn_ref[:, 0:d], wf_ref[...], preferred_element_type=F32)

    acc = jnp.dot(hn_ref[:, 0:d], w_ref[...], preferred_element_type=F32)
    for c in range(acc.shape[1] // LANES):
        qkv_ref[c] = acc[:, c * LANES:(c + 1) * LANES].astype(BF16)


def _inproj(x, g, w, wf, bm=1024, bn=1024):
    m, d = x.shape
    n = w.shape[1]
    return pl.pallas_call(
        _inproj_kernel,
        grid=(m // bm, n // bn),
        in_specs=[
            pl.BlockSpec((bm, d), lambda i, j: (i, 0)),
            pl.BlockSpec((1, d), lambda i, j: (0, 0)),
            pl.BlockSpec((d, bn), lambda i, j: (0, j)),
            pl.BlockSpec((d, LANES), lambda i, j: (0, 0)),
        ],
        out_specs=[
            pl.BlockSpec((bn // LANES, bm, LANES), lambda i, j: (j, i, 0)),
            pl.BlockSpec((bm, LANES), lambda i, j: (i, 0)),
        ],
        out_shape=[
            jax.ShapeDtypeStruct((n // LANES, m, LANES), BF16),
            jax.ShapeDtypeStruct((m, LANES), F32),
        ],
        scratch_shapes=[pltpu.VMEM((bm, d + LHS_PAD), BF16)],
        compiler_params=_params("arbitrary", "arbitrary"),
        name="inproj",
    )(x, g, w, wf)


def _gate_cumsum_kernel(f_ref, b_ref, ccol_ref, crow_ref):
    t = f_ref.shape[0]
    blk = LANES
    r_i = lax.broadcasted_iota(jnp.int32, (blk, blk), 0)
    c_i = lax.broadcasted_iota(jnp.int32, (blk, blk), 1)
    tri = jnp.where(c_i <= r_i, 1.0, 0.0).astype(BF16)
    carry = jnp.zeros((1, LANES), F32)
    for n in range(t // blk):
        lf = _log_sigmoid(f_ref[n * blk:(n + 1) * blk, :] + b_ref[...])
        h1 = lf.astype(BF16)
        r1 = lf - h1.astype(F32)
        h2 = r1.astype(BF16)
        h3 = (r1 - h2.astype(F32)).astype(BF16)
        c = (jnp.dot(tri, h1, preferred_element_type=F32)
             + jnp.dot(tri, h2, preferred_element_type=F32)
             + jnp.dot(tri, h3, preferred_element_type=F32)) + carry
        ccol_ref[n * blk:(n + 1) * blk, :] = c
        crow_ref[:, n * blk:(n + 1) * blk] = jnp.transpose(c)[0:SUBLANES, :]
        carry = c[blk - 1:blk, :]


def _gate_cumsum(f, b_pad, t):
    m = f.shape[0]
    nb = m // t
    return pl.pallas_call(
        _gate_cumsum_kernel,
        grid=(nb,),
        in_specs=[
            pl.BlockSpec((t, LANES), lambda b: (b, 0)),
            pl.BlockSpec((1, LANES), lambda b: (0, 0)),
        ],
        out_specs=[
            pl.BlockSpec((t, LANES), lambda b: (b, 0)),
            pl.BlockSpec((None, SUBLANES, t), lambda b: (b, 0, 0)),
        ],
        out_shape=[
            jax.ShapeDtypeStruct((m, LANES), F32),
            jax.ShapeDtypeStruct((nb, SUBLANES, t), F32),
        ],
        compiler_params=_params("arbitrary"),
        name="gate_cumsum",
    )(f, b_pad)


def _dilated_kernel(slopes_ref, q_ref, k_ref, v_ref, g_ref, o_ref,
                    qf_ref, kf_ref, vf_ref, qd_ref, kd_ref, va_ref, s_ref, p_ref, mx_ref,
                    oacc_ref, lse_ref):
    h = pl.program_id(1)
    t = q_ref.shape[0]
    wb = WINDOW_BLOCK
    slope = slopes_ref[h]
    qf_ref[...] = q_ref[...].astype(F32)
    kf_ref[...] = k_ref[...].astype(F32)
    vf_ref[...] = v_ref[...].astype(F32)
    for p in range(len(DILATION_PATTERNS)):
        kd_ref[p, 0:wb, :] = jnp.zeros((wb, HEAD_DIM), BF16)
        va_ref[p, 0:wb, :] = jnp.zeros((wb, 2 * HEAD_DIM), BF16)
        va_ref[p, wb:, HEAD_DIM:] = jnp.ones((t, HEAD_DIM), BF16)

    q_i = lax.broadcasted_iota(jnp.int32, (wb, 2 * wb), 0)
    k_i = lax.broadcasted_iota(jnp.int32, (wb, 2 * wb), 1)
    delta_i = wb + q_i - k_i
    delta = delta_i.astype(F32)
    in_window = (delta_i >= 0) & (delta_i <= wb)
    rc = 256

    for p, (window, dil) in enumerate(DILATION_PATTERNS):
        assert window // dil == wb and t % (dil * wb) == 0
        seq = t // dil
        nblk = seq // wb
        has_prev = nblk > 1
        bias_band = jnp.where(in_window, (-slope * dil) * delta, NEG)
        bias_first = jnp.where(k_i >= wb, bias_band, NEG)
        for r in range(dil):
            src = pl.ds(r, seq, stride=dil) if dil > 1 else pl.ds(0, seq)
            qd_ref[p, r * seq:(r + 1) * seq, :] = qf_ref[src, :].astype(BF16)
            kd_ref[p, wb + r * seq:wb + (r + 1) * seq, :] = kf_ref[src, :].astype(BF16)
            va_ref[p, wb + r * seq:wb + (r + 1) * seq, 0:HEAD_DIM] = vf_ref[src, :].astype(BF16)
        width = 2 * wb if has_prev else wb
        for blk in range(t // wb):
            rows = slice(blk * wb, (blk + 1) * wb)
            if has_prev:
                band = kd_ref[p, blk * wb:(blk + 2) * wb, :]
                bias = bias_first if blk % nblk == 0 else bias_band
            else:
                band = kd_ref[p, (blk + 1) * wb:(blk + 2) * wb, :]
                bias = bias_band[:, wb:]
            s_ref[rows, 0:width] = _nt_dot(qd_ref[p, rows, :], band) + bias
        for c in range(t // rc):
            rows = slice(c * rc, (c + 1) * rc)
            s = s_ref[rows, 0:width]
            mx = jnp.max(s, axis=-1, keepdims=True)
            p_ref[p, rows, 0:width] = jnp.exp(s - mx).astype(BF16)
            mx_ref[rows, :] = jnp.broadcast_to(mx, (rc, LANES))
        for blk in range(t // wb):
            rows = slice(blk * wb, (blk + 1) * wb)
            vrows = slice(blk * wb, (blk + 2) * wb) if has_prev else slice((blk + 1) * wb, (blk + 2) * wb)
            o2 = jnp.dot(p_ref[p, rows, 0:width], va_ref[p, vrows, :], preferred_element_type=F32)
            l = o2[:, HEAD_DIM:]
            r, n = divmod(blk, nblk)
            dest = pl.ds(r + dil * wb * n, wb, stride=dil) if dil > 1 else pl.ds(blk * wb, wb)
            oacc_ref[p, dest, :] = o2[:, 0:HEAD_DIM] / l
            lse_ref[p, dest, :] = mx_ref[rows, :] + jnp.log(l)

    g = g_ref[pl.ds(h, 1), :]
    for c in range(t // rc):
        rs = slice(c * rc, (c + 1) * rc)
        l0, l1, l2 = lse_ref[0, rs, :], lse_ref[1, rs, :], lse_ref[2, rs, :]
        mx = jnp.maximum(jnp.maximum(l0, l1), l2)
        e0, e1, e2 = jnp.exp(l0 - mx), jnp.exp(l1 - mx), jnp.exp(l2 - mx)
        o = (e0 * oacc_ref[0, rs, :] + e1 * oacc_ref[1, rs, :] + e2 * oacc_ref[2, rs, :]) / (e0 + e1 + e2)
        o_ref[rs, :] = _head_rmsnorm(o, g).astype(BF16)


def _dilated_attention(qkv, g_head, slopes, nb, t):
    m = nb * t
    npat = len(DILATION_PATTERNS)
    wb = WINDOW_BLOCK
    grid_spec = pltpu.PrefetchScalarGridSpec(
        num_scalar_prefetch=1,
        grid=(nb, N_HEADS_A),
        in_specs=[
            pl.BlockSpec((None, t, HEAD_DIM), lambda b, h, s: (QA0 + h, b, 0)),
            pl.BlockSpec((None, t, HEAD_DIM), lambda b, h, s: (KA0 + h, b, 0)),
            pl.BlockSpec((None, t, HEAD_DIM), lambda b, h, s: (VA0 + h, b, 0)),
            pl.BlockSpec((N_HEADS, HEAD_DIM), lambda b, h, s: (0, 0)),
        ],
        out_specs=pl.BlockSpec((t, HEAD_DIM), lambda b, h, s: (b, h)),
        scratch_shapes=[
            pltpu.VMEM((t, HEAD_DIM), F32),
            pltpu.VMEM((t, HEAD_DIM), F32),
            pltpu.VMEM((t, HEAD_DIM), F32),
            pltpu.VMEM((npat, t, HEAD_DIM), BF16),
            pltpu.VMEM((npat, t + wb, HEAD_DIM), BF16),
            pltpu.VMEM((npat, t + wb, 2 * HEAD_DIM), BF16),
            pltpu.VMEM((t, 2 * wb), F32),
            pltpu.VMEM((npat, t, 2 * wb), BF16),
            pltpu.VMEM((t, LANES), F32),
            pltpu.VMEM((npat, t, HEAD_DIM), F32),
            pltpu.VMEM((npat, t, LANES), F32),
        ],
    )
    return pl.pallas_call(
        _dilated_kernel,
        grid_spec=grid_spec,
        out_shape=jax.ShapeDtypeStruct((m, N_HEADS_A * HEAD_DIM), BF16),
        compiler_params=_params("arbitrary", "arbitrary"),
        name="dilated_attention",
    )(slopes, qkv, qkv, qkv, g_head)


def _stickbreak_kernel(q_ref, k_ref, v_ref, g_ref, o_ref, a_ref):
    j = pl.program_id(1)
    t = q_ref.shape[0]
    bq, ck = ATT_BQ, ATT_CHUNK
    assert bq == ck
    g = g_ref[pl.ds(N_HEADS_A + j, 1), :]
    r_i = lax.broadcasted_iota(jnp.int32, (ck, ck), 0)
    c_i = lax.broadcasted_iota(jnp.int32, (ck, ck), 1)
    suffix = jnp.where(r_i > c_i, 1.0, 0.0).astype(BF16)
    suffix2 = jnp.concatenate([suffix, suffix], axis=0)
    causal = c_i < r_i
    for n in range(t // bq):
        q = q_ref[n * bq:(n + 1) * bq, :]
        later = jnp.zeros((bq, 1), F32)
        for c in range(n, -1, -1):
            z = _nt_dot(q, k_ref[c * ck:(c + 1) * ck, :])
            log_1m = jnp.minimum(-z, 0.0) - jnp.log(1.0 + jnp.exp(-jnp.abs(z)))
            log_sig = z + log_1m
            if c == n:
                log_1m = jnp.where(causal, log_1m, 0.0)
            hi = log_1m.astype(BF16)
            lo = (log_1m - hi.astype(F32)).astype(BF16)
            after = jnp.dot(jnp.concatenate([hi, lo], axis=1), suffix2,
                            preferred_element_type=F32) + later
            a = jnp.exp(log_sig + after)
            if c == n:
                a = jnp.where(causal, a, 0.0)
            a_ref[n, :, c * ck:(c + 1) * ck] = a.astype(BF16)
            if c > 0:
                later = later + jnp.sum(log_1m, axis=-1, keepdims=True)
        ext = (n + 1) * bq
        o = jnp.dot(a_ref[n, :, 0:ext], v_ref[0:ext, :], preferred_element_type=F32)
        o_ref[n * bq:(n + 1) * bq, :] = _head_rmsnorm(o, g).astype(BF16)


def _stickbreak_attention(qkv, g_head, nb, t):
    m = nb * t
    return pl.pallas_call(
        _stickbreak_kernel,
        grid=(nb, N_HEADS_B),
        in_specs=[
            pl.BlockSpec((None, t, HEAD_DIM), lambda b, j: (QB0 + j, b, 0)),
            pl.BlockSpec((None, t, HEAD_DIM), lambda b, j: (KB0 + j, b, 0)),
            pl.BlockSpec((None, t, HEAD_DIM), lambda b, j: (VB0 + j, b, 0)),
            pl.BlockSpec((N_HEADS, HEAD_DIM), lambda b, j: (0, 0)),
        ],
        out_specs=pl.BlockSpec((t, HEAD_DIM), lambda b, j: (b, j)),
        out_shape=jax.ShapeDtypeStruct((m, N_HEADS_B * HEAD_DIM), BF16),
        scratch_shapes=[pltpu.VMEM((t // ATT_BQ, ATT_BQ, t), BF16)],
        compiler_params=_params("arbitrary", "arbitrary"),
        name="stickbreak_attention",
    )(qkv, qkv, qkv, g_head)


def _forgetting_kernel(q_ref, k_ref, v_ref, ccol_ref, crow_ref, g_ref, o_ref, s_ref, p_ref):
    j = pl.program_id(1)
    t = q_ref.shape[0]
    bq, ck = ATT_BQ, FOX_CHUNK
    g = g_ref[pl.ds(N_HEADS_A + N_HEADS_B + j, 1), :]
    lane = lax.broadcasted_iota(jnp.int32, (1, LANES), 1)
    r_i = lax.broadcasted_iota(jnp.int32, (bq, bq), 0)
    c_i = lax.broadcasted_iota(jnp.int32, (bq, bq), 1)
    causal = c_i <= r_i
    for n in range(t // bq):
        q0, ext = n * bq, (n + 1) * bq
        q = q_ref[q0:ext, :]
        c_t = jnp.sum(jnp.where(lane == j, ccol_ref[q0:ext, :], 0.0), axis=-1, keepdims=True)
        mx = None
        for k0 in range(0, ext, ck):
            k1 = min(k0 + ck, ext)
            s = _nt_dot(q, k_ref[k0:k1, :]) + c_t - crow_ref[pl.ds(j, 1), k0:k1]
            if k1 == ext:
                split = k1 - k0 - bq
                diag = jnp.where(causal, s[:, split:], NEG)
                s = diag if split == 0 else jnp.concatenate([s[:, :split], diag], axis=1)
            s_ref[:, k0:k1] = s
            cm = jnp.max(s, axis=-1, keepdims=True)
            mx = cm if mx is None else jnp.maximum(mx, cm)
        l = jnp.zeros((bq, 1), F32)
        for k0 in range(0, ext, ck):
            k1 = min(k0 + ck, ext)
            p = jnp.exp(s_ref[:, k0:k1] - mx)
            l = l + jnp.sum(p, axis=-1, keepdims=True)
            p_ref[n, :, k0:k1] = p.astype(BF16)
        o = jnp.dot(p_ref[n, :, 0:ext], v_ref[0:ext, :], preferred_element_type=F32) / l
        o_ref[q0:ext, :] = _head_rmsnorm(o, g).astype(BF16)


def _forgetting_attention(qkv, ccol, crow, g_head, nb, t):
    m = nb * t
    return pl.pallas_call(
        _forgetting_kernel,
        grid=(nb, N_HEADS_C),
        in_specs=[
            pl.BlockSpec((None, t, HEAD_DIM), lambda b, j: (QC0 + j, b, 0)),
            pl.BlockSpec((None, t, HEAD_DIM), lambda b, j: (KC0 + j, b, 0)),
            pl.BlockSpec((None, t, HEAD_DIM), lambda b, j: (VC0 + j, b, 0)),
            pl.BlockSpec((t, LANES), lambda b, j: (b, 0)),
            pl.BlockSpec((None, SUBLANES, t), lambda b, j: (b, 0, 0)),
            pl.BlockSpec((N_HEADS, HEAD_DIM), lambda b, j: (0, 0)),
        ],
        out_specs=pl.BlockSpec((t, HEAD_DIM), lambda b, j: (b, j)),
        out_shape=jax.ShapeDtypeStruct((m, N_HEADS_C * HEAD_DIM), BF16),
        scratch_shapes=[pltpu.VMEM((ATT_BQ, t), F32), pltpu.VMEM((t // ATT_BQ, ATT_BQ, t), BF16)],
        compiler_params=_params("arbitrary", "arbitrary"),
        name="forgetting_attention",
    )(qkv, qkv, qkv, ccol, crow, g_head)


def _outproj_kernel(x_ref, oa_ref, ob_ref, oc_ref, wa_ref, wb_ref, wc_ref, g_ref, x1_ref, hn_ref):
    x1 = (x_ref[...]
          + jnp.dot(oa_ref[...], wa_ref[...], preferred_element_type=F32)
          + jnp.dot(ob_ref[...], wb_ref[...], preferred_element_type=F32)
          + jnp.dot(oc_ref[...], wc_ref[...], preferred_element_type=F32))
    x1_ref[...] = x1
    d = x1.shape[1]
    y = x1 * lax.rsqrt(jnp.mean(x1 * x1, axis=-1, keepdims=True) + EPS)
    hn_ref[:, 0:d] = (y * g_ref[...]).astype(BF16)
    hn_ref[:, d:] = jnp.zeros((x1.shape[0], hn_ref.shape[1] - d), BF16)


def _outproj(x, oa, ob, oc, wa, wb, wc, g, bm=512):
    m, d = x.shape
    row = lambda i: (i, 0)
    whole = lambda i: (0, 0)
    return pl.pallas_call(
        _outproj_kernel,
        grid=(m // bm,),
        in_specs=[
            pl.BlockSpec((bm, d), row),
            pl.BlockSpec((bm, oa.shape[1]), row),
            pl.BlockSpec((bm, ob.shape[1]), row),
            pl.BlockSpec((bm, oc.shape[1]), row),
            pl.BlockSpec(wa.shape, whole),
            pl.BlockSpec(wb.shape, whole),
            pl.BlockSpec(wc.shape, whole),
            pl.BlockSpec((1, d), whole),
        ],
        out_specs=[pl.BlockSpec((bm, d), row), pl.BlockSpec((bm, d + LHS_PAD), row)],
        out_shape=[jax.ShapeDtypeStruct((m, d), F32), jax.ShapeDtypeStruct((m, d + LHS_PAD), BF16)],
        compiler_params=_params("arbitrary"),
        name="outproj",
    )(x, oa, ob, oc, wa, wb, wc, g)


def _ffn_up_kernel(hn_ref, wg_ref, wu_ref, cwg_ref, cwu_ref, cbg_ref, cbu_ref, h_ref,
                   w16_ref, halo_ref, raw0_ref, raw1_ref, *, ni, tiles_per_seq, cw, mp, ep):
    s = pl.program_id(0)
    last = pl.num_programs(0) - 2
    i = lax.rem(jnp.minimum(s, last), ni)
    ip = lax.rem(jnp.maximum(s - 1, 0), ni)
    bm, (d, bf) = hn_ref.shape[0], wg_ref.shape
    pad = SUBLANES
    chunks = [(slice(c * cw, (c + 1) * cw),
               slice(2 * c * cw, (2 * c + 1) * cw),
               slice((2 * c + 1) * cw, (2 * c + 2) * cw))
              for c in range(bf // cw)]

    @pl.when(i == 0)
    def _():
        for cols, gcols, ucols in chunks:
            w16_ref[:, gcols] = wg_ref[:, cols].astype(BF16)
            w16_ref[:, ucols] = wu_ref[:, cols].astype(BF16)

    @pl.when(s == 0)
    def _():
        raw1_ref[...] = jnp.zeros(raw1_ref.shape, F32)

    @pl.when(lax.rem(ip, tiles_per_seq) == 0)
    def _():
        halo_ref[...] = jnp.zeros(halo_ref.shape, F32)

    def step(cur_ref, prev_ref):
        prev_ref[0:pad, :] = halo_ref[...]
        halo_ref[...] = prev_ref[bm:bm + pad, :]
        for cols, gcols, ucols in chunks:
            both = slice(gcols.start, ucols.stop)
            for m0 in range(0, bm, mp):
                cur_ref[pad + m0:pad + m0 + mp, both] = jnp.dot(
                    hn_ref[m0:m0 + mp, 0:d], w16_ref[:, both], preferred_element_type=F32)
                for r0 in range(m0, m0 + mp, ep):

                    def conv(ccols, cw_ref, cb_ref):
                        return (cb_ref[:, cols]
                                + cw_ref[0:1, cols] * prev_ref[pad - 2 + r0:pad - 2 + r0 + ep, ccols]
                                + cw_ref[1:2, cols] * prev_ref[pad - 1 + r0:pad - 1 + r0 + ep, ccols]
                                + cw_ref[2:3, cols] * prev_ref[pad + r0:pad + r0 + ep, ccols])

                    gate_c = conv(gcols, cwg_ref, cbg_ref)
                    up_c = conv(ucols, cwu_ref, cbu_ref)
                    h_ref[r0:r0 + ep, cols] = (
                        gate_c * (1.0 / (1.0 + jnp.exp(-gate_c))) * up_c).astype(BF16)

    @pl.when(lax.rem(s, 2) == 0)
    def _():
        step(raw0_ref, raw1_ref)

    @pl.when(lax.rem(s, 2) == 1)
    def _():
        step(raw1_ref, raw0_ref)


def _ffn_up(hn, w_gu, conv_w, conv_b, t, bm=1024, bf=512, cw=256, mp=256, ep=64):
    m, dp = hn.shape
    d = w_gu.shape[0]
    dff = w_gu.shape[1] // 2
    ni, nj = m // bm, dff // bf
    n = ni * nj
    assert t % bm == 0 and bf % cw == 0

    def mm_tile(s):
        q = jnp.minimum(s, n - 1)
        return lax.rem(q, ni), lax.div(q, ni)

    def ep_tile(s):
        q = jnp.maximum(s - 1, 0)
        return lax.rem(q, ni), lax.div(q, ni)

    raw = pltpu.VMEM((bm + SUBLANES, 2 * bf), F32)
    return pl.pallas_call(
        functools.partial(_ffn_up_kernel, ni=ni, tiles_per_seq=t // bm, cw=cw, mp=mp, ep=ep),
        grid=(n + 1,),
        in_specs=[
            pl.BlockSpec((bm, dp), lambda s: (mm_tile(s)[0], 0)),
            pl.BlockSpec((d, bf), lambda s: (0, mm_tile(s)[1])),
            pl.BlockSpec((d, bf), lambda s: (0, nj + mm_tile(s)[1])),
            pl.BlockSpec((CONV_WIDTH, bf), lambda s: (0, ep_tile(s)[1])),
            pl.BlockSpec((CONV_WIDTH, bf), lambda s: (0, nj + ep_tile(s)[1])),
            pl.BlockSpec((1, bf), lambda s: (0, ep_tile(s)[1])),
            pl.BlockSpec((1, bf), lambda s: (0, nj + ep_tile(s)[1])),
        ],
        out_specs=pl.BlockSpec((bm, bf), lambda s: ep_tile(s)),
        out_shape=jax.ShapeDtypeStruct((m, dff), BF16),
        scratch_shapes=[
            pltpu.VMEM((d, 2 * bf), BF16),
            pltpu.VMEM((SUBLANES, 2 * bf), F32),
            raw, raw,
        ],
        compiler_params=_params("arbitrary"),
        name="ffn_up",
    )(hn, w_gu, w_gu, conv_w, conv_w, conv_b, conv_b)


def _ffn_down_kernel(x_ref, h_ref, w_ref, o_ref):
    o_ref[...] = x_ref[...] + jnp.dot(h_ref[...], w_ref[...], preferred_element_type=F32)


def _ffn_down(x, h, w, bm=1024, bn=512):
    m, d = x.shape
    k = h.shape[1]
    return pl.pallas_call(
        _ffn_down_kernel,
        grid=(m // bm, d // bn),
        in_specs=[
            pl.BlockSpec((bm, bn), lambda i, j: (i, j)),
            pl.BlockSpec((bm, k), lambda i, j: (i, 0)),
            pl.BlockSpec((k, bn), lambda i, j: (0, j)),
        ],
        out_specs=pl.BlockSpec((bm, bn), lambda i, j: (i, j)),
        out_shape=jax.ShapeDtypeStruct((m, d), F32),
        compiler_params=_params("arbitrary", "arbitrary"),
        name="ffn_down",
    )(x, h, w)


def _rmsnorm_kernel(x_ref, g_ref, o_ref):
    x = x_ref[...]
    y = x * lax.rsqrt(jnp.mean(x * x, axis=-1, keepdims=True) + EPS)
    o_ref[...] = y * g_ref[...]


def _rmsnorm(x, g, bm=1024):
    m, d = x.shape
    return pl.pallas_call(
        _rmsnorm_kernel,
        grid=(m // bm,),
        in_specs=[pl.BlockSpec((bm, d), lambda i: (i, 0)), pl.BlockSpec((1, d), lambda i: (0, 0))],
        out_specs=pl.BlockSpec((bm, d), lambda i: (i, 0)),
        out_shape=jax.ShapeDtypeStruct((m, d), F32),
        compiler_params=_params("arbitrary"),
        name="final_rmsnorm",
    )(x, g)


def _layer(x, nb, t, g_mix, w_in, b_f, g_head, w_o, g_ffn, w_gu, conv_w, conv_b, w_down, slopes):
    d = x.shape[1]
    n_qkv = N_QKV_BLOCKS * HEAD_DIM
    scale = 1.0 / math.sqrt(HEAD_DIM)
    blk = jnp.arange(N_QKV_BLOCKS)
    is_q = ((blk < KA0) | ((blk >= QB0) & (blk < KB0)) | ((blk >= QC0) & (blk < KC0)))
    col_scale = jnp.repeat(jnp.where(is_q, scale, 1.0).astype(F32), HEAD_DIM)
    w_qkv = (w_in[:, :n_qkv] * col_scale[None, :]).astype(BF16)
    w_f = jnp.pad(w_in[:, n_qkv:], ((0, 0), (0, LANES - N_HEADS_C))).astype(BF16)
    b_pad = jnp.pad(b_f, (0, LANES - N_HEADS_C)).reshape(1, LANES)

    qkv, f = _inproj(x, g_mix.reshape(1, d), w_qkv, w_f)
    ccol, crow = _gate_cumsum(f, b_pad, t)
    o_a = _dilated_attention(qkv, g_head, slopes, nb, t)
    o_b = _stickbreak_attention(qkv, g_head, nb, t)
    o_c = _forgetting_attention(qkv, ccol, crow, g_head, nb, t)

    w_o16 = w_o.astype(BF16)
    ka, kb = N_HEADS_A * HEAD_DIM, (N_HEADS_A + N_HEADS_B) * HEAD_DIM
    x1, hn = _outproj(x, o_a, o_b, o_c, w_o16[:ka], w_o16[ka:kb], w_o16[kb:], g_ffn.reshape(1, d))
    h = _ffn_up(hn, w_gu, conv_w, conv_b.reshape(1, -1), t)
    return _ffn_down(x1, h, w_down.astype(BF16))


def kernel(x, g_mix, w_in, b_f, g_head, w_o, g_ffn, w_gu, conv_w, conv_b, w_down, g_final):
    nb, t, d = x.shape
    depth = g_mix.shape[0]
    slopes = 2.0 ** (-8.0 * (jnp.arange(N_HEADS_A, dtype=F32) + 1.0) / N_HEADS_A)
    xf = x.reshape(nb * t, d)
    for layer in range(depth):
        xf = _layer(xf, nb, t, g_mix[layer], w_in[layer], b_f[layer], g_head[layer], w_o[layer],
                    g_ffn[layer], w_gu[layer], conv_w[layer], conv_b[layer], w_down[layer], slopes)
    return _rmsnorm(xf, g_final.reshape(1, d)).reshape(nb, t, d)
```

```python
import functools
import math

import jax
import jax.numpy as jnp
from jax import lax
from jax.experimental import pallas as pl
from jax.experimental.pallas import tpu as pltpu

F32 = jnp.float32
BF16 = jnp.bfloat16

HEAD_DIM = 128
N_HEADS_A = 6
N_HEADS_B = 5
N_HEADS_C = 5
N_HEADS = N_HEADS_A + N_HEADS_B + N_HEADS_C
DILATION_PATTERNS = ((128, 1), (512, 4), (2048, 16))
WINDOW_BLOCK = 128
CONV_WIDTH = 3
EPS = 1e-6
NEG = -1e30
LANES = 128
SUBLANES = 8
VMEM_LIMIT_BYTES = 56 * 1024 * 1024
LHS_PAD = LANES
LOG2_E = 1.4426950408889634

QA0, KA0, VA0 = 0, N_HEADS_A, 2 * N_HEADS_A
QB0 = 3 * N_HEADS_A
KB0, VB0 = QB0 + N_HEADS_B, QB0 + 2 * N_HEADS_B
QC0 = QB0 + 3 * N_HEADS_B
KC0, VC0 = QC0 + N_HEADS_C, QC0 + 2 * N_HEADS_C
N_QKV_BLOCKS = QC0 + 3 * N_HEADS_C

ATT_BQ = 256
ATT_CHUNK = 256
FOX_CHUNK = 512


def _params(*sem):
    return pltpu.CompilerParams(dimension_semantics=sem, vmem_limit_bytes=VMEM_LIMIT_BYTES)


def _nt_dot(a, b):
    return lax.dot_general(a, b, (((1,), (1,)), ((), ())), preferred_element_type=F32)


def _head_rmsnorm(o, g):
    return o * lax.rsqrt(jnp.mean(o * o, axis=-1, keepdims=True) + EPS) * g


def _log_sigmoid(x):
    return jnp.minimum(x, 0.0) - jnp.log(1.0 + jnp.exp(-jnp.abs(x)))


def _cast_kernel(w_ref, s_ref, o_ref):
    o_ref[...] = (w_ref[...] * s_ref[...]).astype(BF16)


def _cast_weights(w, col_scale, bk):
    nl, k, _ = w.shape
    n = col_scale.shape[1]
    return pl.pallas_call(
        _cast_kernel,
        grid=(nl, k // bk),
        in_specs=[
            pl.BlockSpec((None, bk, n), lambda l, i: (l, i, 0)),
            pl.BlockSpec((1, n), lambda l, i: (0, 0)),
        ],
        out_specs=pl.BlockSpec((None, bk, n), lambda l, i: (l, i, 0)),
        out_shape=jax.ShapeDtypeStruct((nl, k, n), BF16),
        compiler_params=_params("arbitrary", "arbitrary"),
        name="cast_weights",
    )(w, col_scale)


def _inproj_kernel(x_ref, g_ref, w_ref, wf_ref, qkv_ref, f_ref, hn_ref):
    d = x_ref.shape[1]

    @pl.when(pl.program_id(1) == 0)
    def _():
        x = x_ref[...]
        y = x * lax.rsqrt(jnp.mean(x * x, axis=-1, keepdims=True) + EPS)
        hn_ref[:, 0:d] = (y * g_ref[...]).astype(BF16)
        f_ref[...] = jnp.dot(hn_ref[:, 0:d], wf_ref[...], preferred_element_type=F32)

    acc = jnp.dot(hn_ref[:, 0:d], w_ref[...], preferred_element_type=F32)
    for c in range(acc.shape[1] // LANES):
        qkv_ref[c] = acc[:, c * LANES:(c + 1) * LANES].astype(BF16)


def _inproj(x, g, w, wf, layer, bm=1024, bn=1024):
    m, d = x.shape
    n = w.shape[2]
    return pl.pallas_call(
        _inproj_kernel,
        grid=(m // bm, n // bn),
        in_specs=[
            pl.BlockSpec((bm, d), lambda i, j: (i, 0)),
            pl.BlockSpec((1, d), lambda i, j: (0, 0)),
            pl.BlockSpec((None, d, bn), lambda i, j: (layer, 0, j)),
            pl.BlockSpec((d, LANES), lambda i, j: (0, 0)),
        ],
        out_specs=[
            pl.BlockSpec((bn // LANES, bm, LANES), lambda i, j: (j, i, 0)),
            pl.BlockSpec((bm, LANES), lambda i, j: (i, 0)),
        ],
        out_shape=[
            jax.ShapeDtypeStruct((n // LANES, m, LANES), BF16),
            jax.ShapeDtypeStruct((m, LANES), F32),
        ],
        scratch_shapes=[pltpu.VMEM((bm, d + LHS_PAD), BF16)],
        compiler_params=_params("arbitrary", "arbitrary"),
        name="inproj",
    )(x, g, w, wf)


def _gate_cumsum_kernel(f_ref, b_ref, ccol_ref, crow_ref):
    t = f_ref.shape[0]
    blk = LANES
    r_i = lax.broadcasted_iota(jnp.int32, (blk, blk), 0)
    c_i = lax.broadcasted_iota(jnp.int32, (blk, blk), 1)
    tri = jnp.where(c_i <= r_i, 1.0, 0.0).astype(BF16)
    carry = jnp.zeros((1, LANES), F32)
    for n in range(t // blk):
        lf = _log_sigmoid(f_ref[n * blk:(n + 1) * blk, :] + b_ref[...])
        h1 = lf.astype(BF16)
        r1 = lf - h1.astype(F32)
        h2 = r1.astype(BF16)
        h3 = (r1 - h2.astype(F32)).astype(BF16)
        c = (jnp.dot(tri, h1, preferred_element_type=F32)
             + jnp.dot(tri, h2, preferred_element_type=F32)
             + jnp.dot(tri, h3, preferred_element_type=F32)) + carry
        ccol_ref[n * blk:(n + 1) * blk, :] = c
        crow_ref[:, n * blk:(n + 1) * blk] = jnp.transpose(c)[0:SUBLANES, :]
        carry = c[blk - 1:blk, :]


def _gate_cumsum(f, b_pad, t):
    m = f.shape[0]
    nb = m // t
    return pl.pallas_call(
        _gate_cumsum_kernel,
        grid=(nb,),
        in_specs=[
            pl.BlockSpec((t, LANES), lambda b: (b, 0)),
            pl.BlockSpec((1, LANES), lambda b: (0, 0)),
        ],
        out_specs=[
            pl.BlockSpec((t, LANES), lambda b: (b, 0)),
            pl.BlockSpec((None, SUBLANES, t), lambda b: (b, 0, 0)),
        ],
        out_shape=[
            jax.ShapeDtypeStruct((m, LANES), F32),
            jax.ShapeDtypeStruct((nb, SUBLANES, t), F32),
        ],
        compiler_params=_params("arbitrary"),
        name="gate_cumsum",
    )(f, b_pad)


def _dilated_kernel(slopes_ref, q_ref, k_ref, v_ref, g_ref, o_ref,
                    qf_ref, kf_ref, vf_ref, qd_ref, kd_ref, va_ref, s_ref, p_ref, mx_ref,
                    oacc_ref, lse_ref):
    h = pl.program_id(1)
    t = q_ref.shape[0]
    wb = WINDOW_BLOCK
    slope = slopes_ref[h]
    qf_ref[...] = q_ref[...].astype(F32)
    kf_ref[...] = k_ref[...].astype(F32)
    vf_ref[...] = v_ref[...].astype(F32)
    for p in range(len(DILATION_PATTERNS)):
        kd_ref[p, 0:wb, :] = jnp.zeros((wb, HEAD_DIM), BF16)
        va_ref[p, 0:wb, :] = jnp.zeros((wb, 2 * HEAD_DIM), BF16)
        va_ref[p, wb:, HEAD_DIM:] = jnp.ones((t, HEAD_DIM), BF16)

    q_i = lax.broadcasted_iota(jnp.int32, (wb, 2 * wb), 0)
    k_i = lax.broadcasted_iota(jnp.int32, (wb, 2 * wb), 1)
    delta_i = wb + q_i - k_i
    delta = delta_i.astype(F32)
    in_window = (delta_i >= 0) & (delta_i <= wb)
    rc = 256

    for p, (window, dil) in enumerate(DILATION_PATTERNS):
        assert window // dil == wb and t % (dil * wb) == 0
        seq = t // dil
        nblk = seq // wb
        has_prev = nblk > 1
        bias_band = jnp.where(in_window, (-slope * dil) * delta, NEG)
        bias_first = jnp.where(k_i >= wb, bias_band, NEG)
        for r in range(dil):
            src = pl.ds(r, seq, stride=dil) if dil > 1 else pl.ds(0, seq)
            qd_ref[p, r * seq:(r + 1) * seq, :] = qf_ref[src, :].astype(BF16)
            kd_ref[p, wb + r * seq:wb + (r + 1) * seq, :] = kf_ref[src, :].astype(BF16)
            va_ref[p, wb + r * seq:wb + (r + 1) * seq, 0:HEAD_DIM] = vf_ref[src, :].astype(BF16)
        width = 2 * wb if has_prev else wb
        for blk in range(t // wb):
            rows = slice(blk * wb, (blk + 1) * wb)
            if has_prev:
                band = kd_ref[p, blk * wb:(blk + 2) * wb, :]
                bias = bias_first if blk % nblk == 0 else bias_band
            else:
                band = kd_ref[p, (blk + 1) * wb:(blk + 2) * wb, :]
                bias = bias_band[:, wb:]
            s_ref[rows, 0:width] = _nt_dot(qd_ref[p, rows, :], band) + bias
        for c in range(t // rc):
            rows = slice(c * rc, (c + 1) * rc)
            s = s_ref[rows, 0:width]
            mx = jnp.max(s, axis=-1, keepdims=True)
            p_ref[p, rows, 0:width] = jnp.exp(s - mx).astype(BF16)
            mx_ref[rows, :] = jnp.broadcast_to(mx, (rc, LANES))
        for blk in range(t // wb):
            rows = slice(blk * wb, (blk + 1) * wb)
            vrows = slice(blk * wb, (blk + 2) * wb) if has_prev else slice((blk + 1) * wb, (blk + 2) * wb)
            o2 = jnp.dot(p_ref[p, rows, 0:width], va_ref[p, vrows, :], preferred_element_type=F32)
            l = o2[:, HEAD_DIM:]
            r, n = divmod(blk, nblk)
            dest = pl.ds(r + dil * wb * n, wb, stride=dil) if dil > 1 else pl.ds(blk * wb, wb)
            oacc_ref[p, dest, :] = o2[:, 0:HEAD_DIM] / l
            lse_ref[p, dest, :] = mx_ref[rows, :] + jnp.log(l)

    g = g_ref[pl.ds(h, 1), :]
    for c in range(t // rc):
        rs = slice(c * rc, (c + 1) * rc)
        l0, l1, l2 = lse_ref[0, rs, :], lse_ref[1, rs, :], lse_ref[2, rs, :]
        mx = jnp.maximum(jnp.maximum(l0, l1), l2)
        e0, e1, e2 = jnp.exp(l0 - mx), jnp.exp(l1 - mx), jnp.exp(l2 - mx)
        o = (e0 * oacc_ref[0, rs, :] + e1 * oacc_ref[1, rs, :] + e2 * oacc_ref[2, rs, :]) / (e0 + e1 + e2)
        o_ref[rs, :] = _head_rmsnorm(o, g).astype(BF16)


def _dilated_attention(qkv, g_head, slopes, nb, t):
    m = nb * t
    npat = len(DILATION_PATTERNS)
    wb = WINDOW_BLOCK
    grid_spec = pltpu.PrefetchScalarGridSpec(
        num_scalar_prefetch=1,
        grid=(nb, N_HEADS_A),
        in_specs=[
            pl.BlockSpec((None, t, HEAD_DIM), lambda b, h, s: (QA0 + h, b, 0)),
            pl.BlockSpec((None, t, HEAD_DIM), lambda b, h, s: (KA0 + h, b, 0)),
            pl.BlockSpec((None, t, HEAD_DIM), lambda b, h, s: (VA0 + h, b, 0)),
            pl.BlockSpec((N_HEADS, HEAD_DIM), lambda b, h, s: (0, 0)),
        ],
        out_specs=pl.BlockSpec((t, HEAD_DIM), lambda b, h, s: (b, h)),
        scratch_shapes=[
            pltpu.VMEM((t, HEAD_DIM), F32),
            pltpu.VMEM((t, HEAD_DIM), F32),
            pltpu.VMEM((t, HEAD_DIM), F32),
            pltpu.VMEM((npat, t, HEAD_DIM), BF16),
            pltpu.VMEM((npat, t + wb, HEAD_DIM), BF16),
            pltpu.VMEM((npat, t + wb, 2 * HEAD_DIM), BF16),
            pltpu.VMEM((t, 2 * wb), F32),
            pltpu.VMEM((npat, t, 2 * wb), BF16),
            pltpu.VMEM((t, LANES), F32),
            pltpu.VMEM((npat, t, HEAD_DIM), F32),
            pltpu.VMEM((npat, t, LANES), F32),
        ],
    )
    return pl.pallas_call(
        _dilated_kernel,
        grid_spec=grid_spec,
        out_shape=jax.ShapeDtypeStruct((m, N_HEADS_A * HEAD_DIM), BF16),
        compiler_params=_params("arbitrary", "arbitrary"),
        name="dilated_attention",
    )(slopes, qkv, qkv, qkv, g_head)


def _stickbreak_kernel(q_ref, k_ref, v_ref, g_ref, o_ref, a_ref):
    j = pl.program_id(1)
    t = q_ref.shape[0]
    bq, ck = ATT_BQ, ATT_CHUNK
    assert bq == ck
    g = g_ref[pl.ds(N_HEADS_A + j, 1), :]
    r_i = lax.broadcasted_iota(jnp.int32, (ck, ck), 0)
    c_i = lax.broadcasted_iota(jnp.int32, (ck, ck), 1)
    suffix = jnp.where(r_i > c_i, 1.0, 0.0).astype(BF16)
    suffix2 = jnp.concatenate([suffix, suffix], axis=0)
    causal = c_i < r_i
    for n in range(t // bq):
        q = q_ref[n * bq:(n + 1) * bq, :]
        later = jnp.zeros((bq, 1), F32)
        for c in range(n, -1, -1):
            z2 = _nt_dot(q, k_ref[c * ck:(c + 1) * ck, :])
            log_1m = jnp.minimum(-z2, 0.0) - jnp.log2(1.0 + jnp.exp2(-jnp.abs(z2)))
            log_sig = z2 + log_1m
            if c == n:
                log_1m = jnp.where(causal, log_1m, 0.0)
            hi = log_1m.astype(BF16)
            lo = (log_1m - hi.astype(F32)).astype(BF16)
            after = jnp.dot(jnp.concatenate([hi, lo], axis=1), suffix2,
                            preferred_element_type=F32) + later
            a = jnp.exp2(log_sig + after)
            if c == n:
                a = jnp.where(causal, a, 0.0)
            a_ref[n, :, c * ck:(c + 1) * ck] = a.astype(BF16)
            if c > 0:
                later = later + jnp.sum(log_1m, axis=-1, keepdims=True)
        ext = (n + 1) * bq
        o = jnp.dot(a_ref[n, :, 0:ext], v_ref[0:ext, :], preferred_element_type=F32)
        o_ref[n * bq:(n + 1) * bq, :] = _head_rmsnorm(o, g).astype(BF16)


def _stickbreak_attention(qkv, g_head, nb, t):
    m = nb * t
    return pl.pallas_call(
        _stickbreak_kernel,
        grid=(nb, N_HEADS_B),
        in_specs=[
            pl.BlockSpec((None, t, HEAD_DIM), lambda b, j: (QB0 + j, b, 0)),
            pl.BlockSpec((None, t, HEAD_DIM), lambda b, j: (KB0 + j, b, 0)),
            pl.BlockSpec((None, t, HEAD_DIM), lambda b, j: (VB0 + j, b, 0)),
            pl.BlockSpec((N_HEADS, HEAD_DIM), lambda b, j: (0, 0)),
        ],
        out_specs=pl.BlockSpec((t, HEAD_DIM), lambda b, j: (b, j)),
        out_shape=jax.ShapeDtypeStruct((m, N_HEADS_B * HEAD_DIM), BF16),
        scratch_shapes=[pltpu.VMEM((t // ATT_BQ, ATT_BQ, t), BF16)],
        compiler_params=_params("arbitrary", "arbitrary"),
        name="stickbreak_attention",
    )(qkv, qkv, qkv, g_head)


def _forgetting_kernel(q_ref, k_ref, v_ref, ccol_ref, crow_ref, g_ref, o_ref, s_ref, p_ref):
    j = pl.program_id(1)
    t = q_ref.shape[0]
    bq, ck = ATT_BQ, FOX_CHUNK
    g = g_ref[pl.ds(N_HEADS_A + N_HEADS_B + j, 1), :]
    lane = lax.broadcasted_iota(jnp.int32, (1, LANES), 1)
    r_i = lax.broadcasted_iota(jnp.int32, (bq, bq), 0)
    c_i = lax.broadcasted_iota(jnp.int32, (bq, bq), 1)
    causal = c_i <= r_i
    for n in range(t // bq):
        q0, ext = n * bq, (n + 1) * bq
        q = q_ref[q0:ext, :]
        c_t = jnp.sum(jnp.where(lane == j, ccol_ref[q0:ext, :], 0.0), axis=-1, keepdims=True)
        mx = None
        for k0 in range(0, ext, ck):
            k1 = min(k0 + ck, ext)
            s = _nt_dot(q, k_ref[k0:k1, :]) + c_t - crow_ref[pl.ds(j, 1), k0:k1]
            if k1 == ext:
                split = k1 - k0 - bq
                diag = jnp.where(causal, s[:, split:], NEG)
                s = diag if split == 0 else jnp.concatenate([s[:, :split], diag], axis=1)
            s_ref[:, k0:k1] = s
            cm = jnp.max(s, axis=-1, keepdims=True)
            mx = cm if mx is None else jnp.maximum(mx, cm)
        l = jnp.zeros((bq, 1), F32)
        for k0 in range(0, ext, ck):
            k1 = min(k0 + ck, ext)
            p = jnp.exp(s_ref[:, k0:k1] - mx)
            l = l + jnp.sum(p, axis=-1, keepdims=True)
            p_ref[n, :, k0:k1] = p.astype(BF16)
        o = jnp.dot(p_ref[n, :, 0:ext], v_ref[0:ext, :], preferred_element_type=F32) / l
        o_ref[q0:ext, :] = _head_rmsnorm(o, g).astype(BF16)


def _forgetting_attention(qkv, ccol, crow, g_head, nb, t):
    m = nb * t
    return pl.pallas_call(
        _forgetting_kernel,
        grid=(nb, N_HEADS_C),
        in_specs=[
            pl.BlockSpec((None, t, HEAD_DIM), lambda b, j: (QC0 + j, b, 0)),
            pl.BlockSpec((None, t, HEAD_DIM), lambda b, j: (KC0 + j, b, 0)),
            pl.BlockSpec((None, t, HEAD_DIM), lambda b, j: (VC0 + j, b, 0)),
            pl.BlockSpec((t, LANES), lambda b, j: (b, 0)),
            pl.BlockSpec((None, SUBLANES, t), lambda b, j: (b, 0, 0)),
            pl.BlockSpec((N_HEADS, HEAD_DIM), lambda b, j: (0, 0)),
        ],
        out_specs=pl.BlockSpec((t, HEAD_DIM), lambda b, j: (b, j)),
        out_shape=jax.ShapeDtypeStruct((m, N_HEADS_C * HEAD_DIM), BF16),
        scratch_shapes=[pltpu.VMEM((ATT_BQ, t), F32), pltpu.VMEM((t // ATT_BQ, ATT_BQ, t), BF16)],
        compiler_params=_params("arbitrary", "arbitrary"),
        name="forgetting_attention",
    )(qkv, qkv, qkv, ccol, crow, g_head)


def _outproj_kernel(x_ref, oa_ref, ob_ref, oc_ref, w_ref, g_ref, x1_ref, hn_ref):
    ka = oa_ref.shape[1]
    kb = ka + ob_ref.shape[1]
    x1 = (x_ref[...]
          + jnp.dot(oa_ref[...], w_ref[0:ka, :], preferred_element_type=F32)
          + jnp.dot(ob_ref[...], w_ref[ka:kb, :], preferred_element_type=F32)
          + jnp.dot(oc_ref[...], w_ref[kb:, :], preferred_element_type=F32))
    x1_ref[...] = x1
    d = x1.shape[1]
    y = x1 * lax.rsqrt(jnp.mean(x1 * x1, axis=-1, keepdims=True) + EPS)
    hn_ref[:, 0:d] = (y * g_ref[...]).astype(BF16)
    hn_ref[:, d:] = jnp.zeros((x1.shape[0], hn_ref.shape[1] - d), BF16)


def _outproj(x, oa, ob, oc, w, g, layer, bm=512):
    m, d = x.shape
    row = lambda i: (i, 0)
    whole = lambda i: (0, 0)
    return pl.pallas_call(
        _outproj_kernel,
        grid=(m // bm,),
        in_specs=[
            pl.BlockSpec((bm, d), row),
            pl.BlockSpec((bm, oa.shape[1]), row),
            pl.BlockSpec((bm, ob.shape[1]), row),
            pl.BlockSpec((bm, oc.shape[1]), row),
            pl.BlockSpec((None, d, d), lambda i: (layer, 0, 0)),
            pl.BlockSpec((1, d), whole),
        ],
        out_specs=[pl.BlockSpec((bm, d), row), pl.BlockSpec((bm, d + LHS_PAD), row)],
        out_shape=[jax.ShapeDtypeStruct((m, d), F32), jax.ShapeDtypeStruct((m, d + LHS_PAD), BF16)],
        compiler_params=_params("arbitrary"),
        name="outproj",
    )(x, oa, ob, oc, w, g)


def _ffn_up_kernel(hn_ref, wg_ref, wu_ref, cwg_ref, cwu_ref, cbg_ref, cbu_ref, h_ref,
                   w16_ref, halo_ref, raw0_ref, raw1_ref, *, ni, tiles_per_seq, cw, mp, ep):
    s = pl.program_id(0)
    last = pl.num_programs(0) - 2
    i = lax.rem(jnp.minimum(s, last), ni)
    ip = lax.rem(jnp.maximum(s - 1, 0), ni)
    bm, (d, bf) = hn_ref.shape[0], wg_ref.shape
    pad = SUBLANES
    chunks = [(slice(c * cw, (c + 1) * cw),
               slice(2 * c * cw, (2 * c + 1) * cw),
               slice((2 * c + 1) * cw, (2 * c + 2) * cw))
              for c in range(bf // cw)]

    @pl.when(i == 0)
    def _():
        for cols, gcols, ucols in chunks:
            w16_ref[:, gcols] = wg_ref[:, cols].astype(BF16)
            w16_ref[:, ucols] = wu_ref[:, cols].astype(BF16)

    @pl.when(s == 0)
    def _():
        raw1_ref[...] = jnp.zeros(raw1_ref.shape, F32)

    @pl.when(lax.rem(ip, tiles_per_seq) == 0)
    def _():
        halo_ref[...] = jnp.zeros(halo_ref.shape, F32)

    def step(cur_ref, prev_ref):
        prev_ref[0:pad, :] = halo_ref[...]
        halo_ref[...] = prev_ref[bm:bm + pad, :]
        for cols, gcols, ucols in chunks:
            both = slice(gcols.start, ucols.stop)
            for m0 in range(0, bm, mp):
                cur_ref[pad + m0:pad + m0 + mp, both] = jnp.dot(
                    hn_ref[m0:m0 + mp, 0:d], w16_ref[:, both], preferred_element_type=F32)
                for r0 in range(m0, m0 + mp, ep):

                    def conv(ccols, cw_ref, cb_ref):
                        return (cb_ref[:, cols]
                                + cw_ref[0:1, cols] * prev_ref[pad - 2 + r0:pad - 2 + r0 + ep, ccols]
                                + cw_ref[1:2, cols] * prev_ref[pad - 1 + r0:pad - 1 + r0 + ep, ccols]
                                + cw_ref[2:3, cols] * prev_ref[pad + r0:pad + r0 + ep, ccols])

                    gate_c = conv(gcols, cwg_ref, cbg_ref)
                    up_c = conv(ucols, cwu_ref, cbu_ref)
                    h_ref[r0:r0 + ep, cols] = (
                        gate_c * (1.0 / (1.0 + jnp.exp(-gate_c))) * up_c).astype(BF16)

    @pl.when(lax.rem(s, 2) == 0)
    def _():
        step(raw0_ref, raw1_ref)

    @pl.when(lax.rem(s, 2) == 1)
    def _():
        step(raw1_ref, raw0_ref)


def _ffn_up(hn, w_gu, conv_w, conv_b, t, layer, bm=1024, bf=512, cw=256, mp=1024, ep=64):
    m, dp = hn.shape
    d = w_gu.shape[1]
    dff = w_gu.shape[2] // 2
    ni, nj = m // bm, dff // bf
    n = ni * nj
    assert t % bm == 0 and bf % cw == 0

    def mm_tile(s):
        q = jnp.minimum(s, n - 1)
        return lax.rem(q, ni), lax.div(q, ni)

    def ep_tile(s):
        q = jnp.maximum(s - 1, 0)
        return lax.rem(q, ni), lax.div(q, ni)

    raw = pltpu.VMEM((bm + SUBLANES, 2 * bf), F32)
    return pl.pallas_call(
        functools.partial(_ffn_up_kernel, ni=ni, tiles_per_seq=t // bm, cw=cw, mp=mp, ep=ep),
        grid=(n + 1,),
        in_specs=[
            pl.BlockSpec((bm, dp), lambda s: (mm_tile(s)[0], 0)),
            pl.BlockSpec((None, d, bf), lambda s: (layer, 0, mm_tile(s)[1])),
            pl.BlockSpec((None, d, bf), lambda s: (layer, 0, nj + mm_tile(s)[1])),
            pl.BlockSpec((None, CONV_WIDTH, bf), lambda s: (layer, 0, ep_tile(s)[1])),
            pl.BlockSpec((None, CONV_WIDTH, bf), lambda s: (layer, 0, nj + ep_tile(s)[1])),
            pl.BlockSpec((None, 1, bf), lambda s: (layer, 0, ep_tile(s)[1])),
            pl.BlockSpec((None, 1, bf), lambda s: (layer, 0, nj + ep_tile(s)[1])),
        ],
        out_specs=pl.BlockSpec((bm, bf), lambda s: ep_tile(s)),
        out_shape=jax.ShapeDtypeStruct((m, dff), BF16),
        scratch_shapes=[
            pltpu.VMEM((d, 2 * bf), BF16),
            pltpu.VMEM((SUBLANES, 2 * bf), F32),
            raw, raw,
        ],
        compiler_params=_params("arbitrary"),
        name="ffn_up",
    )(hn, w_gu, w_gu, conv_w, conv_w, conv_b, conv_b)


def _ffn_down_kernel(x_ref, h_ref, w_ref, o_ref):
    o_ref[...] = x_ref[...] + jnp.dot(h_ref[...], w_ref[...], preferred_element_type=F32)


def _ffn_down(x, h, w, layer, bm=1024, bn=512):
    m, d = x.shape
    k = h.shape[1]
    return pl.pallas_call(
        _ffn_down_kernel,
        grid=(m // bm, d // bn),
        in_specs=[
            pl.BlockSpec((bm, bn), lambda i, j: (i, j)),
            pl.BlockSpec((bm, k), lambda i, j: (i, 0)),
            pl.BlockSpec((None, k, bn), lambda i, j: (layer, 0, j)),
        ],
        out_specs=pl.BlockSpec((bm, bn), lambda i, j: (i, j)),
        out_shape=jax.ShapeDtypeStruct((m, d), F32),
        compiler_params=_params("arbitrary", "arbitrary"),
        name="ffn_down",
    )(x, h, w)


def _rmsnorm_kernel(x_ref, g_ref, o_ref):
    x = x_ref[...]
    y = x * lax.rsqrt(jnp.mean(x * x, axis=-1, keepdims=True) + EPS)
    o_ref[...] = y * g_ref[...]


def _rmsnorm(x, g, bm=1024):
    m, d = x.shape
    return pl.pallas_call(
        _rmsnorm_kernel,
        grid=(m // bm,),
        in_specs=[pl.BlockSpec((bm, d), lambda i: (i, 0)), pl.BlockSpec((1, d), lambda i: (0, 0))],
        out_specs=pl.BlockSpec((bm, d), lambda i: (i, 0)),
        out_shape=jax.ShapeDtypeStruct((m, d), F32),
        compiler_params=_params("arbitrary"),
        name="final_rmsnorm",
    )(x, g)


def _layer(x, nb, t, layer, g_mix, w_qkv, w_f, b_f, g_head, w_o, g_ffn, w_gu, conv_w, conv_b, w_down,
           slopes):
    d = x.shape[1]
    b_pad = jnp.pad(b_f, (0, LANES - N_HEADS_C)).reshape(1, LANES)
    qkv, f = _inproj(x, g_mix.reshape(1, d), w_qkv, w_f, layer)
    ccol, crow = _gate_cumsum(f, b_pad, t)
    o_a = _dilated_attention(qkv, g_head, slopes, nb, t)
    o_b = _stickbreak_attention(qkv, g_head, nb, t)
    o_c = _forgetting_attention(qkv, ccol, crow, g_head, nb, t)
    x1, hn = _outproj(x, o_a, o_b, o_c, w_o, g_ffn.reshape(1, d), layer)
    h = _ffn_up(hn, w_gu, conv_w, conv_b, t, layer)
    return _ffn_down(x1, h, w_down, layer)


def _query_column_scale():
    scale = 1.0 / math.sqrt(HEAD_DIM)
    blk = jnp.arange(N_QKV_BLOCKS)
    is_qb = (blk >= QB0) & (blk < KB0)
    is_q = (blk < KA0) | is_qb | ((blk >= QC0) & (blk < KC0))
    per_block = jnp.where(is_qb, scale * LOG2_E, jnp.where(is_q, scale, 1.0)).astype(F32)
    return jnp.repeat(per_block, HEAD_DIM).reshape(1, -1)


def kernel(x, g_mix, w_in, b_f, g_head, w_o, g_ffn, w_gu, conv_w, conv_b, w_down, g_final):
    nb, t, d = x.shape
    depth = g_mix.shape[0]
    n_qkv = N_QKV_BLOCKS * HEAD_DIM
    slopes = 2.0 ** (-8.0 * (jnp.arange(N_HEADS_A, dtype=F32) + 1.0) / N_HEADS_A)
    w_qkv = _cast_weights(w_in, _query_column_scale(), bk=256)
    w_f = jnp.pad(w_in[:, :, n_qkv:], ((0, 0), (0, 0), (0, LANES - N_HEADS_C))).astype(BF16)
    w_o16 = _cast_weights(w_o, jnp.ones((1, d), F32), bk=1024)
    w_down16 = _cast_weights(w_down, jnp.ones((1, d), F32), bk=w_down.shape[1] // 8)
    conv_b3 = conv_b.reshape(depth, 1, -1)
    xf = x.reshape(nb * t, d)
    for layer in range(depth):
        xf = _layer(xf, nb, t, layer, g_mix[layer], w_qkv, w_f[layer], b_f[layer], g_head[layer], w_o16,
                    g_ffn[layer], w_gu, conv_w, conv_b3, w_down16, slopes)
    return _rmsnorm(xf, g_final.reshape(1, d)).reshape(nb, t, d)
```

```python
import functools
import math

import jax
import jax.numpy as jnp
from jax import lax
from jax.experimental import pallas as pl
from jax.experimental.pallas import tpu as pltpu

F32 = jnp.float32
BF16 = jnp.bfloat16

HEAD_DIM = 128
N_HEADS_A = 6
N_HEADS_B = 5
N_HEADS_C = 5
N_HEADS = N_HEADS_A + N_HEADS_B + N_HEADS_C
DILATION_PATTERNS = ((128, 1), (512, 4), (2048, 16))
WINDOW_BLOCK = 128
CONV_WIDTH = 3
EPS = 1e-6
NEG = -1e30
LANES = 128
SUBLANES = 8
VMEM_LIMIT_BYTES = 56 * 1024 * 1024
LHS_PAD = LANES
LOG2_E = 1.4426950408889634

QA0, KA0, VA0 = 0, N_HEADS_A, 2 * N_HEADS_A
QB0 = 3 * N_HEADS_A
KB0, VB0 = QB0 + N_HEADS_B, QB0 + 2 * N_HEADS_B
QC0 = QB0 + 3 * N_HEADS_B
KC0, VC0 = QC0 + N_HEADS_C, QC0 + 2 * N_HEADS_C
N_QKV_BLOCKS = QC0 + 3 * N_HEADS_C

ATT_BQ = 256
ATT_CHUNK = 256
FOX_CHUNK = 512


def _params(*sem):
    return pltpu.CompilerParams(dimension_semantics=sem, vmem_limit_bytes=VMEM_LIMIT_BYTES)


def _nt_dot(a, b):
    return lax.dot_general(a, b, (((1,), (1,)), ((), ())), preferred_element_type=F32)


def _head_rmsnorm(o, g):
    return o * lax.rsqrt(jnp.mean(o * o, axis=-1, keepdims=True) + EPS) * g


def _log_sigmoid(x):
    return jnp.minimum(x, 0.0) - jnp.log(1.0 + jnp.exp(-jnp.abs(x)))


def _cast_kernel(w_ref, s_ref, o_ref):
    o_ref[...] = (w_ref[...] * s_ref[...]).astype(BF16)


def _cast_weights(w, col_scale, bk):
    nl, k, _ = w.shape
    n = col_scale.shape[1]
    return pl.pallas_call(
        _cast_kernel,
        grid=(nl, k // bk),
        in_specs=[
            pl.BlockSpec((None, bk, n), lambda l, i: (l, i, 0)),
            pl.BlockSpec((1, n), lambda l, i: (0, 0)),
        ],
        out_specs=pl.BlockSpec((None, bk, n), lambda l, i: (l, i, 0)),
        out_shape=jax.ShapeDtypeStruct((nl, k, n), BF16),
        compiler_params=_params("arbitrary", "arbitrary"),
        name="cast_weights",
    )(w, col_scale)


def _inproj_kernel(x_ref, g_ref, w_ref, wf_ref, qkv_ref, f_ref, hn_ref):
    d = x_ref.shape[1]

    @pl.when(pl.program_id(1) == 0)
    def _():
        x = x_ref[...]
        y = x * lax.rsqrt(jnp.mean(x * x, axis=-1, keepdims=True) + EPS)
        hn_ref[:, 0:d] = (y * g_ref[...]).astype(BF16)
        f_ref[...] = jnp.dot(hn_ref[:, 0:d], wf_ref[...], preferred_element_type=F32)

    acc = jnp.dot(hn_ref[:, 0:d], w_ref[...], preferred_element_type=F32)
    for c in range(acc.shape[1] // LANES):
        qkv_ref[c] = acc[:, c * LANES:(c + 1) * LANES].astype(BF16)


def _inproj(x, g, w, wf, layer, bm=1024, bn=1024):
    m, d = x.shape
    n = w.shape[2]
    return pl.pallas_call(
        _inproj_kernel,
        grid=(m // bm, n // bn),
        in_specs=[
            pl.BlockSpec((bm, d), lambda i, j: (i, 0)),
            pl.BlockSpec((1, d), lambda i, j: (0, 0)),
            pl.BlockSpec((None, d, bn), lambda i, j: (layer, 0, j)),
            pl.BlockSpec((d, LANES), lambda i, j: (0, 0)),
        ],
        out_specs=[
            pl.BlockSpec((bn // LANES, bm, LANES), lambda i, j: (j, i, 0)),
            pl.BlockSpec((bm, LANES), lambda i, j: (i, 0)),
        ],
        out_shape=[
            jax.ShapeDtypeStruct((n // LANES, m, LANES), BF16),
            jax.ShapeDtypeStruct((m, LANES), F32),
        ],
        scratch_shapes=[pltpu.VMEM((bm, d + LHS_PAD), BF16)],
        compiler_params=_params("arbitrary", "arbitrary"),
        name="inproj",
    )(x, g, w, wf)


def _gate_cumsum_kernel(f_ref, b_ref, ccol_ref, crow_ref):
    t = f_ref.shape[0]
    blk = LANES
    r_i = lax.broadcasted_iota(jnp.int32, (blk, blk), 0)
    c_i = lax.broadcasted_iota(jnp.int32, (blk, blk), 1)
    tri = jnp.where(c_i <= r_i, 1.0, 0.0).astype(BF16)
    carry = jnp.zeros((1, LANES), F32)
    for n in range(t // blk):
        lf = _log_sigmoid(f_ref[n * blk:(n + 1) * blk, :] + b_ref[...])
        h1 = lf.astype(BF16)
        r1 = lf - h1.astype(F32)
        h2 = r1.astype(BF16)
        h3 = (r1 - h2.astype(F32)).astype(BF16)
        c = (jnp.dot(tri, h1, preferred_element_type=F32)
             + jnp.dot(tri, h2, preferred_element_type=F32)
             + jnp.dot(tri, h3, preferred_element_type=F32)) + carry
        ccol_ref[n * blk:(n + 1) * blk, :] = c
        crow_ref[:, n * blk:(n + 1) * blk] = jnp.transpose(c)[0:SUBLANES, :]
        carry = c[blk - 1:blk, :]


def _gate_cumsum(f, b_pad, t):
    m = f.shape[0]
    nb = m // t
    return pl.pallas_call(
        _gate_cumsum_kernel,
        grid=(nb,),
        in_specs=[
            pl.BlockSpec((t, LANES), lambda b: (b, 0)),
            pl.BlockSpec((1, LANES), lambda b: (0, 0)),
        ],
        out_specs=[
            pl.BlockSpec((t, LANES), lambda b: (b, 0)),
            pl.BlockSpec((None, SUBLANES, t), lambda b: (b, 0, 0)),
        ],
        out_shape=[
            jax.ShapeDtypeStruct((m, LANES), F32),
            jax.ShapeDtypeStruct((nb, SUBLANES, t), F32),
        ],
        compiler_params=_params("arbitrary"),
        name="gate_cumsum",
    )(f, b_pad)


def _dilated_kernel(slopes_ref, q_ref, k_ref, v_ref, g_ref, o_ref,
                    qf_ref, kf_ref, vf_ref, qd_ref, kd_ref, va_ref, s_ref, p_ref, mx_ref,
                    oacc_ref, lse_ref):
    h = pl.program_id(1)
    t = q_ref.shape[0]
    wb = WINDOW_BLOCK
    slope = slopes_ref[h]
    qf_ref[...] = q_ref[...].astype(F32)
    kf_ref[...] = k_ref[...].astype(F32)
    vf_ref[...] = v_ref[...].astype(F32)
    for p in range(len(DILATION_PATTERNS)):
        kd_ref[p, 0:wb, :] = jnp.zeros((wb, HEAD_DIM), BF16)
        va_ref[p, 0:wb, :] = jnp.zeros((wb, 2 * HEAD_DIM), BF16)
        va_ref[p, wb:, HEAD_DIM:] = jnp.ones((t, HEAD_DIM), BF16)

    q_i = lax.broadcasted_iota(jnp.int32, (wb, 2 * wb), 0)
    k_i = lax.broadcasted_iota(jnp.int32, (wb, 2 * wb), 1)
    delta_i = wb + q_i - k_i
    delta = delta_i.astype(F32)
    in_window = (delta_i >= 0) & (delta_i <= wb)
    rc = 256

    for p, (window, dil) in enumerate(DILATION_PATTERNS):
        assert window // dil == wb and t % (dil * wb) == 0
        seq = t // dil
        nblk = seq // wb
        has_prev = nblk > 1
        bias_band = jnp.where(in_window, (-slope * dil) * delta, NEG)
        bias_first = jnp.where(k_i >= wb, bias_band, NEG)
        for r in range(dil):
            src = pl.ds(r, seq, stride=dil) if dil > 1 else pl.ds(0, seq)
            qd_ref[p, r * seq:(r + 1) * seq, :] = qf_ref[src, :].astype(BF16)
            kd_ref[p, wb + r * seq:wb + (r + 1) * seq, :] = kf_ref[src, :].astype(BF16)
            va_ref[p, wb + r * seq:wb + (r + 1) * seq, 0:HEAD_DIM] = vf_ref[src, :].astype(BF16)
        width = 2 * wb if has_prev else wb
        for blk in range(t // wb):
            rows = slice(blk * wb, (blk + 1) * wb)
            if has_prev:
                band = kd_ref[p, blk * wb:(blk + 2) * wb, :]
                bias = bias_first if blk % nblk == 0 else bias_band
            else:
                band = kd_ref[p, (blk + 1) * wb:(blk + 2) * wb, :]
                bias = bias_band[:, wb:]
            s_ref[rows, 0:width] = _nt_dot(qd_ref[p, rows, :], band) + bias
        for c in range(t // rc):
            rows = slice(c * rc, (c + 1) * rc)
            s = s_ref[rows, 0:width]
            mx = jnp.max(s, axis=-1, keepdims=True)
            p_ref[p, rows, 0:width] = jnp.exp(s - mx).astype(BF16)
            mx_ref[rows, :] = jnp.broadcast_to(mx, (rc, LANES))
        for blk in range(t // wb):
            rows = slice(blk * wb, (blk + 1) * wb)
            vrows = slice(blk * wb, (blk + 2) * wb) if has_prev else slice((blk + 1) * wb, (blk + 2) * wb)
            o2 = jnp.dot(p_ref[p, rows, 0:width], va_ref[p, vrows, :], preferred_element_type=F32)
            l = o2[:, HEAD_DIM:]
            r, n = divmod(blk, nblk)
            dest = pl.ds(r + dil * wb * n, wb, stride=dil) if dil > 1 else pl.ds(blk * wb, wb)
            oacc_ref[p, dest, :] = o2[:, 0:HEAD_DIM] / l
            lse_ref[p, dest, :] = mx_ref[rows, :] + jnp.log(l)

    g = g_ref[pl.ds(h, 1), :]
    for c in range(t // rc):
        rs = slice(c * rc, (c + 1) * rc)
        l0, l1, l2 = lse_ref[0, rs, :], lse_ref[1, rs, :], lse_ref[2, rs, :]
        mx = jnp.maximum(jnp.maximum(l0, l1), l2)
        e0, e1, e2 = jnp.exp(l0 - mx), jnp.exp(l1 - mx), jnp.exp(l2 - mx)
        o = (e0 * oacc_ref[0, rs, :] + e1 * oacc_ref[1, rs, :] + e2 * oacc_ref[2, rs, :]) / (e0 + e1 + e2)
        o_ref[rs, :] = _head_rmsnorm(o, g).astype(BF16)


def _dilated_attention(qkv, g_head, slopes, nb, t):
    m = nb * t
    npat = len(DILATION_PATTERNS)
    wb = WINDOW_BLOCK
    grid_spec = pltpu.PrefetchScalarGridSpec(
        num_scalar_prefetch=1,
        grid=(nb, N_HEADS_A),
        in_specs=[
            pl.BlockSpec((None, t, HEAD_DIM), lambda b, h, s: (QA0 + h, b, 0)),
            pl.BlockSpec((None, t, HEAD_DIM), lambda b, h, s: (KA0 + h, b, 0)),
            pl.BlockSpec((None, t, HEAD_DIM), lambda b, h, s: (VA0 + h, b, 0)),
            pl.BlockSpec((N_HEADS, HEAD_DIM), lambda b, h, s: (0, 0)),
        ],
        out_specs=pl.BlockSpec((t, HEAD_DIM), lambda b, h, s: (b, h)),
        scratch_shapes=[
            pltpu.VMEM((t, HEAD_DIM), F32),
            pltpu.VMEM((t, HEAD_DIM), F32),
            pltpu.VMEM((t, HEAD_DIM), F32),
            pltpu.VMEM((npat, t, HEAD_DIM), BF16),
            pltpu.VMEM((npat, t + wb, HEAD_DIM), BF16),
            pltpu.VMEM((npat, t + wb, 2 * HEAD_DIM), BF16),
            pltpu.VMEM((t, 2 * wb), F32),
            pltpu.VMEM((npat, t, 2 * wb), BF16),
            pltpu.VMEM((t, LANES), F32),
            pltpu.VMEM((npat, t, HEAD_DIM), F32),
            pltpu.VMEM((npat, t, LANES), F32),
        ],
    )
    return pl.pallas_call(
        _dilated_kernel,
        grid_spec=grid_spec,
        out_shape=jax.ShapeDtypeStruct((m, N_HEADS_A * HEAD_DIM), BF16),
        compiler_params=_params("arbitrary", "arbitrary"),
        name="dilated_attention",
    )(slopes, qkv, qkv, qkv, g_head)


def _stickbreak_kernel(q_ref, k_ref, v_ref, g_ref, o_ref, a_ref, lsig_ref, lsplit_ref):
    j = pl.program_id(1)
    t = q_ref.shape[0]
    bq, ck = ATT_BQ, ATT_CHUNK
    assert bq == ck
    g = g_ref[pl.ds(N_HEADS_A + j, 1), :]
    r_i = lax.broadcasted_iota(jnp.int32, (ck, ck), 0)
    c_i = lax.broadcasted_iota(jnp.int32, (ck, ck), 1)
    suffix = jnp.where(r_i > c_i, 1.0, 0.0).astype(BF16)
    suffix2 = jnp.concatenate([suffix, suffix], axis=0)
    causal = c_i < r_i
    for n in range(t // bq):
        q = q_ref[n * bq:(n + 1) * bq, :]
        pair0 = n * (n + 1) // 2
        row_sum = {}
        for c in range(n + 1):
            z2 = _nt_dot(q, k_ref[c * ck:(c + 1) * ck, :])
            log_1m = jnp.minimum(-z2, 0.0) - jnp.log2(1.0 + jnp.exp2(-jnp.abs(z2)))
            lsig_ref[:, c * ck:(c + 1) * ck] = z2 + log_1m
            if c == n:
                log_1m = jnp.where(causal, log_1m, 0.0)
            hi = log_1m.astype(BF16)
            lo = (log_1m - hi.astype(F32)).astype(BF16)
            lsplit_ref[pair0 + c] = jnp.concatenate([hi, lo], axis=1)
            if c > 0:
                row_sum[c] = jnp.sum(log_1m, axis=-1, keepdims=True)
        later = jnp.zeros((bq, 1), F32)
        for c in range(n, -1, -1):
            after = jnp.dot(lsplit_ref[pair0 + c], suffix2, preferred_element_type=F32) + later
            a = jnp.exp2(lsig_ref[:, c * ck:(c + 1) * ck] + after)
            if c == n:
                a = jnp.where(causal, a, 0.0)
            a_ref[n, :, c * ck:(c + 1) * ck] = a.astype(BF16)
            if c > 0:
                later = later + row_sum[c]
        ext = (n + 1) * bq
        o = jnp.dot(a_ref[n, :, 0:ext], v_ref[0:ext, :], preferred_element_type=F32)
        o_ref[n * bq:(n + 1) * bq, :] = _head_rmsnorm(o, g).astype(BF16)


def _stickbreak_attention(qkv, g_head, nb, t):
    m = nb * t
    return pl.pallas_call(
        _stickbreak_kernel,
        grid=(nb, N_HEADS_B),
        in_specs=[
            pl.BlockSpec((None, t, HEAD_DIM), lambda b, j: (QB0 + j, b, 0)),
            pl.BlockSpec((None, t, HEAD_DIM), lambda b, j: (KB0 + j, b, 0)),
            pl.BlockSpec((None, t, HEAD_DIM), lambda b, j: (VB0 + j, b, 0)),
            pl.BlockSpec((N_HEADS, HEAD_DIM), lambda b, j: (0, 0)),
        ],
        out_specs=pl.BlockSpec((t, HEAD_DIM), lambda b, j: (b, j)),
        out_shape=jax.ShapeDtypeStruct((m, N_HEADS_B * HEAD_DIM), BF16),
        scratch_shapes=[
            pltpu.VMEM((t // ATT_BQ, ATT_BQ, t), BF16),
            pltpu.VMEM((ATT_BQ, t), F32),
            pltpu.VMEM(((t // ATT_BQ) * (t // ATT_BQ + 1) // 2, ATT_BQ, 2 * ATT_CHUNK), BF16),
        ],
        compiler_params=_params("arbitrary", "arbitrary"),
        name="stickbreak_attention",
    )(qkv, qkv, qkv, g_head)


def _forgetting_kernel(q_ref, k_ref, v_ref, ccol_ref, crow_ref, g_ref, o_ref, s_ref, p_ref):
    j = pl.program_id(1)
    t = q_ref.shape[0]
    bq, ck = ATT_BQ, FOX_CHUNK
    g = g_ref[pl.ds(N_HEADS_A + N_HEADS_B + j, 1), :]
    lane = lax.broadcasted_iota(jnp.int32, (1, LANES), 1)
    r_i = lax.broadcasted_iota(jnp.int32, (bq, bq), 0)
    c_i = lax.broadcasted_iota(jnp.int32, (bq, bq), 1)
    causal = c_i <= r_i
    for n in range(t // bq):
        q0, ext = n * bq, (n + 1) * bq
        q = q_ref[q0:ext, :]
        c_t = jnp.sum(jnp.where(lane == j, ccol_ref[q0:ext, :], 0.0), axis=-1, keepdims=True)
        mx = None
        for k0 in range(0, ext, ck):
            k1 = min(k0 + ck, ext)
            s = _nt_dot(q, k_ref[k0:k1, :]) + c_t - crow_ref[pl.ds(j, 1), k0:k1]
            if k1 == ext:
                split = k1 - k0 - bq
                diag = jnp.where(causal, s[:, split:], NEG)
                s = diag if split == 0 else jnp.concatenate([s[:, :split], diag], axis=1)
            s_ref[:, k0:k1] = s
            cm = jnp.max(s, axis=-1, keepdims=True)
            mx = cm if mx is None else jnp.maximum(mx, cm)
        l = jnp.zeros((bq, 1), F32)
        for k0 in range(0, ext, ck):
            k1 = min(k0 + ck, ext)
            p = jnp.exp(s_ref[:, k0:k1] - mx)
            l = l + jnp.sum(p, axis=-1, keepdims=True)
            p_ref[n, :, k0:k1] = p.astype(BF16)
        o = jnp.dot(p_ref[n, :, 0:ext], v_ref[0:ext, :], preferred_element_type=F32) / l
        o_ref[q0:ext, :] = _head_rmsnorm(o, g).astype(BF16)


def _forgetting_attention(qkv, ccol, crow, g_head, nb, t):
    m = nb * t
    return pl.pallas_call(
        _forgetting_kernel,
        grid=(nb, N_HEADS_C),
        in_specs=[
            pl.BlockSpec((None, t, HEAD_DIM), lambda b, j: (QC0 + j, b, 0)),
            pl.BlockSpec((None, t, HEAD_DIM), lambda b, j: (KC0 + j, b, 0)),
            pl.BlockSpec((None, t, HEAD_DIM), lambda b, j: (VC0 + j, b, 0)),
            pl.BlockSpec((t, LANES), lambda b, j: (b, 0)),
            pl.BlockSpec((None, SUBLANES, t), lambda b, j: (b, 0, 0)),
            pl.BlockSpec((N_HEADS, HEAD_DIM), lambda b, j: (0, 0)),
        ],
        out_specs=pl.BlockSpec((t, HEAD_DIM), lambda b, j: (b, j)),
        out_shape=jax.ShapeDtypeStruct((m, N_HEADS_C * HEAD_DIM), BF16),
        scratch_shapes=[pltpu.VMEM((ATT_BQ, t), F32), pltpu.VMEM((t // ATT_BQ, ATT_BQ, t), BF16)],
        compiler_params=_params("arbitrary", "arbitrary"),
        name="forgetting_attention",
    )(qkv, qkv, qkv, ccol, crow, g_head)


def _outproj_kernel(x_ref, oa_ref, ob_ref, oc_ref, w_ref, g_ref, x1_ref, hn_ref):
    ka = oa_ref.shape[1]
    kb = ka + ob_ref.shape[1]
    x1 = (x_ref[...]
          + jnp.dot(oa_ref[...], w_ref[0:ka, :], preferred_element_type=F32)
          + jnp.dot(ob_ref[...], w_ref[ka:kb, :], preferred_element_type=F32)
          + jnp.dot(oc_ref[...], w_ref[kb:, :], preferred_element_type=F32))
    x1_ref[...] = x1
    d = x1.shape[1]
    y = x1 * lax.rsqrt(jnp.mean(x1 * x1, axis=-1, keepdims=True) + EPS)
    hn_ref[:, 0:d] = (y * g_ref[...]).astype(BF16)
    hn_ref[:, d:] = jnp.zeros((x1.shape[0], hn_ref.shape[1] - d), BF16)


def _outproj(x, oa, ob, oc, w, g, layer, bm=512):
    m, d = x.shape
    row = lambda i: (i, 0)
    whole = lambda i: (0, 0)
    return pl.pallas_call(
        _outproj_kernel,
        grid=(m // bm,),
        in_specs=[
            pl.BlockSpec((bm, d), row),
            pl.BlockSpec((bm, oa.shape[1]), row),
            pl.BlockSpec((bm, ob.shape[1]), row),
            pl.BlockSpec((bm, oc.shape[1]), row),
            pl.BlockSpec((None, d, d), lambda i: (layer, 0, 0)),
            pl.BlockSpec((1, d), whole),
        ],
        out_specs=[pl.BlockSpec((bm, d), row), pl.BlockSpec((bm, d + LHS_PAD), row)],
        out_shape=[jax.ShapeDtypeStruct((m, d), F32), jax.ShapeDtypeStruct((m, d + LHS_PAD), BF16)],
        compiler_params=_params("arbitrary"),
        name="outproj",
    )(x, oa, ob, oc, w, g)


def _ffn_up_kernel(hn_ref, wg_ref, wu_ref, cwg_ref, cwu_ref, cbg_ref, cbu_ref, h_ref,
                   w16_ref, halo_ref, raw0_ref, raw1_ref, stage_ref, *, ni, tiles_per_seq, cw, ep):
    s = pl.program_id(0)
    last = pl.num_programs(0) - 2
    i = lax.rem(jnp.minimum(s, last), ni)
    ip = lax.rem(jnp.maximum(s - 1, 0), ni)
    bm, (d, bf) = hn_ref.shape[0], wg_ref.shape
    pad = SUBLANES
    spc = cw // LANES
    half = ep // 2

    @pl.when(i == 0)
    def _():
        for c in range(bf // cw):
            w16_ref[:, 2 * c * cw:(2 * c + 1) * cw] = wg_ref[:, c * cw:(c + 1) * cw].astype(BF16)
            w16_ref[:, (2 * c + 1) * cw:(2 * c + 2) * cw] = wu_ref[:, c * cw:(c + 1) * cw].astype(BF16)

    @pl.when(s == 0)
    def _():
        raw1_ref[...] = jnp.zeros(raw1_ref.shape, F32)

    @pl.when(lax.rem(ip, tiles_per_seq) == 0)
    def _():
        halo_ref[...] = jnp.zeros(halo_ref.shape, F32)

    def step(cur_ref, prev_ref):
        prev_ref[:, 0:pad, :] = halo_ref[...]
        halo_ref[...] = prev_ref[:, bm:bm + pad, :]
        for c in range(bf // cw):
            res = jnp.dot(hn_ref[:, 0:d], w16_ref[:, 2 * c * cw:(2 * c + 2) * cw],
                          preferred_element_type=F32)
            for k in range(2 * spc):
                cur_ref[2 * c * spc + k, pad:, :] = res[:, k * LANES:(k + 1) * LANES]
            for k in range(spc):
                g_slab, u_slab = 2 * c * spc + k, (2 * c + 1) * spc + k
                cols = slice(c * cw + k * LANES, c * cw + (k + 1) * LANES)
                for r0 in range(0, bm, ep):

                    def conv(slab, cw_ref, cb_ref):
                        def rows(first):
                            return prev_ref[slab, pl.ds(pad + r0 + first, half, stride=2), :]
                        w0, w1, w2 = cw_ref[0:1, cols], cw_ref[1:2, cols], cw_ref[2:3, cols]
                        b = cb_ref[:, cols]
                        x_e, x_o, x_em, x_om = rows(0), rows(1), rows(-2), rows(-1)
                        return b + w0 * x_em + w1 * x_om + w2 * x_e, b + w0 * x_om + w1 * x_e + w2 * x_o

                    gate_e, gate_o = conv(g_slab, cwg_ref, cbg_ref)
                    up_e, up_o = conv(u_slab, cwu_ref, cbu_ref)
                    stage_ref[pl.ds(r0, half, stride=2), :] = (
                        gate_e * (1.0 / (1.0 + jnp.exp(-gate_e))) * up_e)
                    stage_ref[pl.ds(r0 + 1, half, stride=2), :] = (
                        gate_o * (1.0 / (1.0 + jnp.exp(-gate_o))) * up_o)
                    h_ref[r0:r0 + ep, cols] = stage_ref[r0:r0 + ep, :].astype(BF16)

    @pl.when(lax.rem(s, 2) == 0)
    def _():
        step(raw0_ref, raw1_ref)

    @pl.when(lax.rem(s, 2) == 1)
    def _():
        step(raw1_ref, raw0_ref)


def _ffn_up(hn, w_gu, conv_w, conv_b, t, layer, bm=1024, bf=512, cw=256, ep=128):
    m, dp = hn.shape
    d = w_gu.shape[1]
    dff = w_gu.shape[2] // 2
    ni, nj = m // bm, dff // bf
    n = ni * nj
    assert t % bm == 0 and bf % cw == 0

    def mm_tile(s):
        q = jnp.minimum(s, n - 1)
        return lax.rem(q, ni), lax.div(q, ni)

    def ep_tile(s):
        q = jnp.maximum(s - 1, 0)
        return lax.rem(q, ni), lax.div(q, ni)

    nslab = 2 * bf // LANES
    raw = pltpu.VMEM((nslab, bm + SUBLANES, LANES), F32)
    return pl.pallas_call(
        functools.partial(_ffn_up_kernel, ni=ni, tiles_per_seq=t // bm, cw=cw, ep=ep),
        grid=(n + 1,),
        in_specs=[
            pl.BlockSpec((bm, dp), lambda s: (mm_tile(s)[0], 0)),
            pl.BlockSpec((None, d, bf), lambda s: (layer, 0, mm_tile(s)[1])),
            pl.BlockSpec((None, d, bf), lambda s: (layer, 0, nj + mm_tile(s)[1])),
            pl.BlockSpec((None, CONV_WIDTH, bf), lambda s: (layer, 0, ep_tile(s)[1])),
            pl.BlockSpec((None, CONV_WIDTH, bf), lambda s: (layer, 0, nj + ep_tile(s)[1])),
            pl.BlockSpec((None, 1, bf), lambda s: (layer, 0, ep_tile(s)[1])),
            pl.BlockSpec((None, 1, bf), lambda s: (layer, 0, nj + ep_tile(s)[1])),
        ],
        out_specs=pl.BlockSpec((bm, bf), lambda s: ep_tile(s)),
        out_shape=jax.ShapeDtypeStruct((m, dff), BF16),
        scratch_shapes=[
            pltpu.VMEM((d, 2 * bf), BF16),
            pltpu.VMEM((nslab, SUBLANES, LANES), F32),
            raw, raw,
            pltpu.VMEM((bm, LANES), F32),
        ],
        compiler_params=_params("arbitrary"),
        name="ffn_up",
    )(hn, w_gu, w_gu, conv_w, conv_w, conv_b, conv_b)


def _ffn_down_kernel(x_ref, h_ref, w_ref, o_ref):
    o_ref[...] = x_ref[...] + jnp.dot(h_ref[...], w_ref[...], preferred_element_type=F32)


def _ffn_down(x, h, w, layer, bm=1024, bn=512):
    m, d = x.shape
    k = h.shape[1]
    return pl.pallas_call(
        _ffn_down_kernel,
        grid=(m // bm, d // bn),
        in_specs=[
            pl.BlockSpec((bm, bn), lambda i, j: (i, j)),
            pl.BlockSpec((bm, k), lambda i, j: (i, 0)),
            pl.BlockSpec((None, k, bn), lambda i, j: (layer, 0, j)),
        ],
        out_specs=pl.BlockSpec((bm, bn), lambda i, j: (i, j)),
        out_shape=jax.ShapeDtypeStruct((m, d), F32),
        compiler_params=_params("arbitrary", "arbitrary"),
        name="ffn_down",
    )(x, h, w)


def _rmsnorm_kernel(x_ref, g_ref, o_ref):
    x = x_ref[...]
    y = x * lax.rsqrt(jnp.mean(x * x, axis=-1, keepdims=True) + EPS)
    o_ref[...] = y * g_ref[...]


def _rmsnorm(x, g, bm=1024):
    m, d = x.shape
    return pl.pallas_call(
        _rmsnorm_kernel,
        grid=(m // bm,),
        in_specs=[pl.BlockSpec((bm, d), lambda i: (i, 0)), pl.BlockSpec((1, d), lambda i: (0, 0))],
        out_specs=pl.BlockSpec((bm, d), lambda i: (i, 0)),
        out_shape=jax.ShapeDtypeStruct((m, d), F32),
        compiler_params=_params("arbitrary"),
        name="final_rmsnorm",
    )(x, g)


def _layer(x, nb, t, layer, g_mix, w_qkv, w_f, b_f, g_head, w_o, g_ffn, w_gu, conv_w, conv_b, w_down,
           slopes):
    d = x.shape[1]
    b_pad = jnp.pad(b_f, (0, LANES - N_HEADS_C)).reshape(1, LANES)
    qkv, f = _inproj(x, g_mix.reshape(1, d), w_qkv, w_f, layer)
    ccol, crow = _gate_cumsum(f, b_pad, t)
    o_a = _dilated_attention(qkv, g_head, slopes, nb, t)
    o_b = _stickbreak_attention(qkv, g_head, nb, t)
    o_c = _forgetting_attention(qkv, ccol, crow, g_head, nb, t)
    x1, hn = _outproj(x, o_a, o_b, o_c, w_o, g_ffn.reshape(1, d), layer)
    h = _ffn_up(hn, w_gu, conv_w, conv_b, t, layer)
    return _ffn_down(x1, h, w_down, layer)


def _query_column_scale():
    scale = 1.0 / math.sqrt(HEAD_DIM)
    blk = jnp.arange(N_QKV_BLOCKS)
    is_qb = (blk >= QB0) & (blk < KB0)
    is_q = (blk < KA0) | is_qb | ((blk >= QC0) & (blk < KC0))
    per_block = jnp.where(is_qb, scale * LOG2_E, jnp.where(is_q, scale, 1.0)).astype(F32)
    return jnp.repeat(per_block, HEAD_DIM).reshape(1, -1)


def kernel(x, g_mix, w_in, b_f, g_head, w_o, g_ffn, w_gu, conv_w, conv_b, w_down, g_final):
    nb, t, d = x.shape
    depth = g_mix.shape[0]
    n_qkv = N_QKV_BLOCKS * HEAD_DIM
    slopes = 2.0 ** (-8.0 * (jnp.arange(N_HEADS_A, dtype=F32) + 1.0) / N_HEADS_A)
    w_qkv = _cast_weights(w_in, _query_column_scale(), bk=256)
    w_f = jnp.pad(w_in[:, :, n_qkv:], ((0, 0), (0, 0), (0, LANES - N_HEADS_C))).astype(BF16)
    w_o16 = _cast_weights(w_o, jnp.ones((1, d), F32), bk=1024)
    w_down16 = _cast_weights(w_down, jnp.ones((1, d), F32), bk=w_down.shape[1] // 8)
    conv_b3 = conv_b.reshape(depth, 1, -1)
    xf = x.reshape(nb * t, d)
    for layer in range(depth):
        xf = _layer(xf, nb, t, layer, g_mix[layer], w_qkv, w_f[layer], b_f[layer], g_head[layer], w_o16,
                    g_ffn[layer], w_gu, conv_w, conv_b3, w_down16, slopes)
    return _rmsnorm(xf, g_final.reshape(1, d)).reshape(nb, t, d)
```

```python
import functools
import math

import jax
import jax.numpy as jnp
from jax import lax
from jax.experimental import pallas as pl
from jax.experimental.pallas import tpu as pltpu

F32 = jnp.float32
BF16 = jnp.bfloat16

HEAD_DIM = 128
N_HEADS_A = 6
N_HEADS_B = 5
N_HEADS_C = 5
N_HEADS = N_HEADS_A + N_HEADS_B + N_HEADS_C
DILATION_PATTERNS = ((128, 1), (512, 4), (2048, 16))
WINDOW_BLOCK = 128
CONV_WIDTH = 3
EPS = 1e-6
NEG = -1e30
LANES = 128
SUBLANES = 8
VMEM_LIMIT_BYTES = 56 * 1024 * 1024
LHS_PAD = LANES
LOG2_E = 1.4426950408889634

QA0, KA0, VA0 = 0, N_HEADS_A, 2 * N_HEADS_A
QB0 = 3 * N_HEADS_A
KB0, VB0 = QB0 + N_HEADS_B, QB0 + 2 * N_HEADS_B
QC0 = QB0 + 3 * N_HEADS_B
KC0, VC0 = QC0 + N_HEADS_C, QC0 + 2 * N_HEADS_C
N_QKV_BLOCKS = QC0 + 3 * N_HEADS_C

ATT_BQ = 256
ATT_CHUNK = 256
FOX_CHUNK = 512


def _params(*sem):
    return pltpu.CompilerParams(dimension_semantics=sem, vmem_limit_bytes=VMEM_LIMIT_BYTES)


def _nt_dot(a, b):
    return lax.dot_general(a, b, (((1,), (1,)), ((), ())), preferred_element_type=F32)


def _head_rmsnorm(o, g):
    return o * lax.rsqrt(jnp.mean(o * o, axis=-1, keepdims=True) + EPS) * g


def _log_sigmoid(x):
    return jnp.minimum(x, 0.0) - jnp.log(1.0 + jnp.exp(-jnp.abs(x)))


def _cast_kernel(w_ref, s_ref, o_ref):
    o_ref[...] = (w_ref[...] * s_ref[...]).astype(BF16)


def _cast_weights(w, col_scale, bk):
    nl, k, _ = w.shape
    n = col_scale.shape[1]
    return pl.pallas_call(
        _cast_kernel,
        grid=(nl, k // bk),
        in_specs=[
            pl.BlockSpec((None, bk, n), lambda l, i: (l, i, 0)),
            pl.BlockSpec((1, n), lambda l, i: (0, 0)),
        ],
        out_specs=pl.BlockSpec((None, bk, n), lambda l, i: (l, i, 0)),
        out_shape=jax.ShapeDtypeStruct((nl, k, n), BF16),
        compiler_params=_params("arbitrary", "arbitrary"),
        name="cast_weights",
    )(w, col_scale)


def _inproj_kernel(x_ref, g_ref, w_ref, wf_ref, qkv_ref, f_ref, hn_ref):
    d = x_ref.shape[1]

    @pl.when(pl.program_id(1) == 0)
    def _():
        x = x_ref[...]
        y = x * lax.rsqrt(jnp.mean(x * x, axis=-1, keepdims=True) + EPS)
        hn_ref[:, 0:d] = (y * g_ref[...]).astype(BF16)
        f_ref[...] = jnp.dot(hn_ref[:, 0:d], wf_ref[...], preferred_element_type=F32)

    acc = jnp.dot(hn_ref[:, 0:d], w_ref[...], preferred_element_type=F32)
    for c in range(acc.shape[1] // LANES):
        qkv_ref[c] = acc[:, c * LANES:(c + 1) * LANES].astype(BF16)


def _inproj(x, g, w, wf, layer, bm=1024, bn=1024):
    m, d = x.shape
    n = w.shape[2]
    return pl.pallas_call(
        _inproj_kernel,
        grid=(m // bm, n // bn),
        in_specs=[
            pl.BlockSpec((bm, d), lambda i, j: (i, 0)),
            pl.BlockSpec((1, d), lambda i, j: (0, 0)),
            pl.BlockSpec((None, d, bn), lambda i, j: (layer, 0, j)),
            pl.BlockSpec((d, LANES), lambda i, j: (0, 0)),
        ],
        out_specs=[
            pl.BlockSpec((bn // LANES, bm, LANES), lambda i, j: (j, i, 0)),
            pl.BlockSpec((bm, LANES), lambda i, j: (i, 0)),
        ],
        out_shape=[
            jax.ShapeDtypeStruct((n // LANES, m, LANES), BF16),
            jax.ShapeDtypeStruct((m, LANES), F32),
        ],
        scratch_shapes=[pltpu.VMEM((bm, d + LHS_PAD), BF16)],
        compiler_params=_params("arbitrary", "arbitrary"),
        name="inproj",
    )(x, g, w, wf)


def _gate_cumsum_kernel(f_ref, b_ref, ccol_ref, crow_ref):
    t = f_ref.shape[0]
    blk = LANES
    r_i = lax.broadcasted_iota(jnp.int32, (blk, blk), 0)
    c_i = lax.broadcasted_iota(jnp.int32, (blk, blk), 1)
    tri = jnp.where(c_i <= r_i, 1.0, 0.0).astype(BF16)
    carry = jnp.zeros((1, LANES), F32)
    for n in range(t // blk):
        lf = _log_sigmoid(f_ref[n * blk:(n + 1) * blk, :] + b_ref[...])
        h1 = lf.astype(BF16)
        r1 = lf - h1.astype(F32)
        h2 = r1.astype(BF16)
        h3 = (r1 - h2.astype(F32)).astype(BF16)
        c = (jnp.dot(tri, h1, preferred_element_type=F32)
             + jnp.dot(tri, h2, preferred_element_type=F32)
             + jnp.dot(tri, h3, preferred_element_type=F32)) + carry
        ccol_ref[n * blk:(n + 1) * blk, :] = c
        crow_ref[:, n * blk:(n + 1) * blk] = jnp.transpose(c)[0:SUBLANES, :]
        carry = c[blk - 1:blk, :]


def _gate_cumsum(f, b_pad, t):
    m = f.shape[0]
    nb = m // t
    return pl.pallas_call(
        _gate_cumsum_kernel,
        grid=(nb,),
        in_specs=[
            pl.BlockSpec((t, LANES), lambda b: (b, 0)),
            pl.BlockSpec((1, LANES), lambda b: (0, 0)),
        ],
        out_specs=[
            pl.BlockSpec((t, LANES), lambda b: (b, 0)),
            pl.BlockSpec((None, SUBLANES, t), lambda b: (b, 0, 0)),
        ],
        out_shape=[
            jax.ShapeDtypeStruct((m, LANES), F32),
            jax.ShapeDtypeStruct((nb, SUBLANES, t), F32),
        ],
        compiler_params=_params("arbitrary"),
        name="gate_cumsum",
    )(f, b_pad)


def _dilated_kernel(slopes_ref, q_ref, k_ref, v_ref, g_ref, o_ref,
                    qf_ref, kf_ref, vf_ref, qd_ref, kd_ref, va_ref, s_ref, p_ref, mx_ref,
                    oacc_ref, lse_ref):
    h = pl.program_id(1)
    t = q_ref.shape[0]
    wb = WINDOW_BLOCK
    slope = slopes_ref[h]
    qf_ref[...] = q_ref[...].astype(F32)
    kf_ref[...] = k_ref[...].astype(F32)
    vf_ref[...] = v_ref[...].astype(F32)
    for p in range(len(DILATION_PATTERNS)):
        kd_ref[p, 0:wb, :] = jnp.zeros((wb, HEAD_DIM), BF16)
        va_ref[p, 0:wb, :] = jnp.zeros((wb, 2 * HEAD_DIM), BF16)
        va_ref[p, wb:, HEAD_DIM:] = jnp.ones((t, HEAD_DIM), BF16)

    q_i = lax.broadcasted_iota(jnp.int32, (wb, 2 * wb), 0)
    k_i = lax.broadcasted_iota(jnp.int32, (wb, 2 * wb), 1)
    delta_i = wb + q_i - k_i
    delta = delta_i.astype(F32)
    in_window = (delta_i >= 0) & (delta_i <= wb)
    rc = 256

    for p, (window, dil) in enumerate(DILATION_PATTERNS):
        assert window // dil == wb and t % (dil * wb) == 0
        seq = t // dil
        nblk = seq // wb
        has_prev = nblk > 1
        bias_band = jnp.where(in_window, (-slope * dil) * delta, NEG)
        bias_first = jnp.where(k_i >= wb, bias_band, NEG)
        for r in range(dil):
            src = pl.ds(r, seq, stride=dil) if dil > 1 else pl.ds(0, seq)
            qd_ref[p, r * seq:(r + 1) * seq, :] = qf_ref[src, :].astype(BF16)
            kd_ref[p, wb + r * seq:wb + (r + 1) * seq, :] = kf_ref[src, :].astype(BF16)
            va_ref[p, wb + r * seq:wb + (r + 1) * seq, 0:HEAD_DIM] = vf_ref[src, :].astype(BF16)
        width = 2 * wb if has_prev else wb
        for blk in range(t // wb):
            rows = slice(blk * wb, (blk + 1) * wb)
            if has_prev:
                band = kd_ref[p, blk * wb:(blk + 2) * wb, :]
                bias = bias_first if blk % nblk == 0 else bias_band
            else:
                band = kd_ref[p, (blk + 1) * wb:(blk + 2) * wb, :]
                bias = bias_band[:, wb:]
            s_ref[rows, 0:width] = _nt_dot(qd_ref[p, rows, :], band) + bias
        for c in range(t // rc):
            rows = slice(c * rc, (c + 1) * rc)
            s = s_ref[rows, 0:width]
            mx = jnp.max(s, axis=-1, keepdims=True)
            p_ref[p, rows, 0:width] = jnp.exp(s - mx).astype(BF16)
            mx_ref[rows, :] = jnp.broadcast_to(mx, (rc, LANES))
        for blk in range(t // wb):
            rows = slice(blk * wb, (blk + 1) * wb)
            vrows = slice(blk * wb, (blk + 2) * wb) if has_prev else slice((blk + 1) * wb, (blk + 2) * wb)
            o2 = jnp.dot(p_ref[p, rows, 0:width], va_ref[p, vrows, :], preferred_element_type=F32)
            l = o2[:, HEAD_DIM:]
            r, n = divmod(blk, nblk)
            dest = pl.ds(r + dil * wb * n, wb, stride=dil) if dil > 1 else pl.ds(blk * wb, wb)
            oacc_ref[p, dest, :] = o2[:, 0:HEAD_DIM] / l
            lse_ref[p, dest, :] = mx_ref[rows, :] + jnp.log(l)

    g = g_ref[pl.ds(h, 1), :]
    for c in range(t // rc):
        rs = slice(c * rc, (c + 1) * rc)
        l0, l1, l2 = lse_ref[0, rs, :], lse_ref[1, rs, :], lse_ref[2, rs, :]
        mx = jnp.maximum(jnp.maximum(l0, l1), l2)
        e0, e1, e2 = jnp.exp(l0 - mx), jnp.exp(l1 - mx), jnp.exp(l2 - mx)
        o = (e0 * oacc_ref[0, rs, :] + e1 * oacc_ref[1, rs, :] + e2 * oacc_ref[2, rs, :]) / (e0 + e1 + e2)
        o_ref[rs, :] = _head_rmsnorm(o, g).astype(BF16)


def _dilated_attention(qkv, g_head, slopes, nb, t):
    m = nb * t
    npat = len(DILATION_PATTERNS)
    wb = WINDOW_BLOCK
    grid_spec = pltpu.PrefetchScalarGridSpec(
        num_scalar_prefetch=1,
        grid=(nb, N_HEADS_A),
        in_specs=[
            pl.BlockSpec((None, t, HEAD_DIM), lambda b, h, s: (QA0 + h, b, 0)),
            pl.BlockSpec((None, t, HEAD_DIM), lambda b, h, s: (KA0 + h, b, 0)),
            pl.BlockSpec((None, t, HEAD_DIM), lambda b, h, s: (VA0 + h, b, 0)),
            pl.BlockSpec((N_HEADS, HEAD_DIM), lambda b, h, s: (0, 0)),
        ],
        out_specs=pl.BlockSpec((t, HEAD_DIM), lambda b, h, s: (b, h)),
        scratch_shapes=[
            pltpu.VMEM((t, HEAD_DIM), F32),
            pltpu.VMEM((t, HEAD_DIM), F32),
            pltpu.VMEM((t, HEAD_DIM), F32),
            pltpu.VMEM((npat, t, HEAD_DIM), BF16),
            pltpu.VMEM((npat, t + wb, HEAD_DIM), BF16),
            pltpu.VMEM((npat, t + wb, 2 * HEAD_DIM), BF16),
            pltpu.VMEM((t, 2 * wb), F32),
            pltpu.VMEM((npat, t, 2 * wb), BF16),
            pltpu.VMEM((t, LANES), F32),
            pltpu.VMEM((npat, t, HEAD_DIM), F32),
            pltpu.VMEM((npat, t, LANES), F32),
        ],
    )
    return pl.pallas_call(
        _dilated_kernel,
        grid_spec=grid_spec,
        out_shape=jax.ShapeDtypeStruct((m, N_HEADS_A * HEAD_DIM), BF16),
        compiler_params=_params("arbitrary", "arbitrary"),
        name="dilated_attention",
    )(slopes, qkv, qkv, qkv, g_head)


def _stickbreak_kernel(q_ref, k_ref, v_ref, g_ref, o_ref, a_ref, lsig_ref, l1m_ref):
    j = pl.program_id(1)
    t = q_ref.shape[0]
    bq, ck = ATT_BQ, ATT_CHUNK
    assert bq == ck
    g = g_ref[pl.ds(N_HEADS_A + j, 1), :]
    r_i = lax.broadcasted_iota(jnp.int32, (ck, ck), 0)
    c_i = lax.broadcasted_iota(jnp.int32, (ck, ck), 1)
    suffix = jnp.where(r_i > c_i, 1.0, 0.0).astype(BF16)
    causal = c_i < r_i
    for n in range(t // bq):
        q = q_ref[n * bq:(n + 1) * bq, :]
        pair0 = n * (n + 1) // 2
        row_sum = {}
        for c in range(n + 1):
            z2 = _nt_dot(q, k_ref[c * ck:(c + 1) * ck, :])
            log_1m = jnp.minimum(-z2, 0.0) - jnp.log2(1.0 + jnp.exp2(-jnp.abs(z2)))
            lsig_ref[:, c * ck:(c + 1) * ck] = z2 + log_1m
            if c == n:
                log_1m = jnp.where(causal, log_1m, 0.0)
            l1m_ref[pair0 + c] = log_1m.astype(BF16)
            if c > 0:
                row_sum[c] = jnp.sum(log_1m, axis=-1, keepdims=True)
        later = jnp.zeros((bq, 1), F32)
        for c in range(n, -1, -1):
            after = jnp.dot(l1m_ref[pair0 + c], suffix, preferred_element_type=F32) + later
            a = jnp.exp2(lsig_ref[:, c * ck:(c + 1) * ck] + after)
            if c == n:
                a = jnp.where(causal, a, 0.0)
            a_ref[n, :, c * ck:(c + 1) * ck] = a.astype(BF16)
            if c > 0:
                later = later + row_sum[c]
        ext = (n + 1) * bq
        o = jnp.dot(a_ref[n, :, 0:ext], v_ref[0:ext, :], preferred_element_type=F32)
        o_ref[n * bq:(n + 1) * bq, :] = _head_rmsnorm(o, g).astype(BF16)


def _stickbreak_attention(qkv, g_head, nb, t):
    m = nb * t
    return pl.pallas_call(
        _stickbreak_kernel,
        grid=(nb, N_HEADS_B),
        in_specs=[
            pl.BlockSpec((None, t, HEAD_DIM), lambda b, j: (QB0 + j, b, 0)),
            pl.BlockSpec((None, t, HEAD_DIM), lambda b, j: (KB0 + j, b, 0)),
            pl.BlockSpec((None, t, HEAD_DIM), lambda b, j: (VB0 + j, b, 0)),
            pl.BlockSpec((N_HEADS, HEAD_DIM), lambda b, j: (0, 0)),
        ],
        out_specs=pl.BlockSpec((t, HEAD_DIM), lambda b, j: (b, j)),
        out_shape=jax.ShapeDtypeStruct((m, N_HEADS_B * HEAD_DIM), BF16),
        scratch_shapes=[
            pltpu.VMEM((t // ATT_BQ, ATT_BQ, t), BF16),
            pltpu.VMEM((ATT_BQ, t), F32),
            pltpu.VMEM(((t // ATT_BQ) * (t // ATT_BQ + 1) // 2, ATT_BQ, ATT_CHUNK), BF16),
        ],
        compiler_params=_params("arbitrary", "arbitrary"),
        name="stickbreak_attention",
    )(qkv, qkv, qkv, g_head)


def _forgetting_kernel(q_ref, k_ref, v_ref, ccol_ref, crow_ref, g_ref, o_ref, s_ref, p_ref):
    j = pl.program_id(1)
    t = q_ref.shape[0]
    bq, ck = ATT_BQ, FOX_CHUNK
    g = g_ref[pl.ds(N_HEADS_A + N_HEADS_B + j, 1), :]
    lane = lax.broadcasted_iota(jnp.int32, (1, LANES), 1)
    r_i = lax.broadcasted_iota(jnp.int32, (bq, bq), 0)
    c_i = lax.broadcasted_iota(jnp.int32, (bq, bq), 1)
    causal = c_i <= r_i
    for n in range(t // bq):
        q0, ext = n * bq, (n + 1) * bq
        q = q_ref[q0:ext, :]
        c_t = jnp.sum(jnp.where(lane == j, ccol_ref[q0:ext, :], 0.0), axis=-1, keepdims=True)
        mx = None
        for k0 in range(0, ext, ck):
            k1 = min(k0 + ck, ext)
            s = _nt_dot(q, k_ref[k0:k1, :]) + c_t - crow_ref[pl.ds(j, 1), k0:k1]
            if k1 == ext:
                split = k1 - k0 - bq
                diag = jnp.where(causal, s[:, split:], NEG)
                s = diag if split == 0 else jnp.concatenate([s[:, :split], diag], axis=1)
            s_ref[:, k0:k1] = s
            cm = jnp.max(s, axis=-1, keepdims=True)
            mx = cm if mx is None else jnp.maximum(mx, cm)
        l = jnp.zeros((bq, 1), F32)
        for k0 in range(0, ext, ck):
            k1 = min(k0 + ck, ext)
            p = jnp.exp(s_ref[:, k0:k1] - mx)
            l = l + jnp.sum(p, axis=-1, keepdims=True)
            p_ref[n, :, k0:k1] = p.astype(BF16)
        o = jnp.dot(p_ref[n, :, 0:ext], v_ref[0:ext, :], preferred_element_type=F32) / l
        o_ref[q0:ext, :] = _head_rmsnorm(o, g).astype(BF16)


def _forgetting_attention(qkv, ccol, crow, g_head, nb, t):
    m = nb * t
    return pl.pallas_call(
        _forgetting_kernel,
        grid=(nb, N_HEADS_C),
        in_specs=[
            pl.BlockSpec((None, t, HEAD_DIM), lambda b, j: (QC0 + j, b, 0)),
            pl.BlockSpec((None, t, HEAD_DIM), lambda b, j: (KC0 + j, b, 0)),
            pl.BlockSpec((None, t, HEAD_DIM), lambda b, j: (VC0 + j, b, 0)),
            pl.BlockSpec((t, LANES), lambda b, j: (b, 0)),
            pl.BlockSpec((None, SUBLANES, t), lambda b, j: (b, 0, 0)),
            pl.BlockSpec((N_HEADS, HEAD_DIM), lambda b, j: (0, 0)),
        ],
        out_specs=pl.BlockSpec((t, HEAD_DIM), lambda b, j: (b, j)),
        out_shape=jax.ShapeDtypeStruct((m, N_HEADS_C * HEAD_DIM), BF16),
        scratch_shapes=[pltpu.VMEM((ATT_BQ, t), F32), pltpu.VMEM((t // ATT_BQ, ATT_BQ, t), BF16)],
        compiler_params=_params("arbitrary", "arbitrary"),
        name="forgetting_attention",
    )(qkv, qkv, qkv, ccol, crow, g_head)


def _outproj_kernel(x_ref, oa_ref, ob_ref, oc_ref, w_ref, g_ref, x1_ref, hn_ref):
    ka = oa_ref.shape[1]
    kb = ka + ob_ref.shape[1]
    x1 = (x_ref[...]
          + jnp.dot(oa_ref[...], w_ref[0:ka, :], preferred_element_type=F32)
          + jnp.dot(ob_ref[...], w_ref[ka:kb, :], preferred_element_type=F32)
          + jnp.dot(oc_ref[...], w_ref[kb:, :], preferred_element_type=F32))
    x1_ref[...] = x1
    d = x1.shape[1]
    y = x1 * lax.rsqrt(jnp.mean(x1 * x1, axis=-1, keepdims=True) + EPS)
    hn_ref[:, 0:d] = (y * g_ref[...]).astype(BF16)
    hn_ref[:, d:] = jnp.zeros((x1.shape[0], hn_ref.shape[1] - d), BF16)


def _outproj(x, oa, ob, oc, w, g, layer, bm=512):
    m, d = x.shape
    row = lambda i: (i, 0)
    whole = lambda i: (0, 0)
    return pl.pallas_call(
        _outproj_kernel,
        grid=(m // bm,),
        in_specs=[
            pl.BlockSpec((bm, d), row),
            pl.BlockSpec((bm, oa.shape[1]), row),
            pl.BlockSpec((bm, ob.shape[1]), row),
            pl.BlockSpec((bm, oc.shape[1]), row),
            pl.BlockSpec((None, d, d), lambda i: (layer, 0, 0)),
            pl.BlockSpec((1, d), whole),
        ],
        out_specs=[pl.BlockSpec((bm, d), row), pl.BlockSpec((bm, d + LHS_PAD), row)],
        out_shape=[jax.ShapeDtypeStruct((m, d), F32), jax.ShapeDtypeStruct((m, d + LHS_PAD), BF16)],
        compiler_params=_params("arbitrary"),
        name="outproj",
    )(x, oa, ob, oc, w, g)


def _ffn_up_kernel(hn_ref, wg_ref, wu_ref, cwg_ref, cwu_ref, cbg_ref, cbu_ref, h_ref,
                   w16_ref, halo_ref, raw0_ref, raw1_ref, stage_ref, *, ni, tiles_per_seq, cw, mp, ep):
    s = pl.program_id(0)
    last = pl.num_programs(0) - 2
    i = lax.rem(jnp.minimum(s, last), ni)
    ip = lax.rem(jnp.maximum(s - 1, 0), ni)
    bm, (d, bf) = hn_ref.shape[0], wg_ref.shape
    pad = SUBLANES
    spc = cw // LANES
    half = ep // 2

    @pl.when(i == 0)
    def _():
        for c in range(bf // cw):
            w16_ref[:, 2 * c * cw:(2 * c + 1) * cw] = wg_ref[:, c * cw:(c + 1) * cw].astype(BF16)
            w16_ref[:, (2 * c + 1) * cw:(2 * c + 2) * cw] = wu_ref[:, c * cw:(c + 1) * cw].astype(BF16)

    @pl.when(s == 0)
    def _():
        raw1_ref[...] = jnp.zeros(raw1_ref.shape, F32)

    @pl.when(lax.rem(ip, tiles_per_seq) == 0)
    def _():
        halo_ref[...] = jnp.zeros(halo_ref.shape, F32)

    def step(cur_ref, prev_ref):
        prev_ref[:, 0:pad, :] = halo_ref[...]
        halo_ref[...] = prev_ref[:, bm:bm + pad, :]
        for c in range(bf // cw):
            for m0 in range(0, bm, mp):
                res = jnp.dot(hn_ref[m0:m0 + mp, 0:d], w16_ref[:, 2 * c * cw:(2 * c + 2) * cw],
                              preferred_element_type=F32)
                for k in range(2 * spc):
                    cur_ref[2 * c * spc + k, pad + m0:pad + m0 + mp, :] = res[:, k * LANES:(k + 1) * LANES]
                for k in range(spc):
                    g_slab, u_slab = 2 * c * spc + k, (2 * c + 1) * spc + k
                    cols = slice(c * cw + k * LANES, c * cw + (k + 1) * LANES)
                    for r0 in range(m0, m0 + mp, ep):

                        def conv(slab, cw_ref, cb_ref):
                            def rows(first):
                                return prev_ref[slab, pl.ds(pad + r0 + first, half, stride=2), :]
                            w0, w1, w2 = cw_ref[0:1, cols], cw_ref[1:2, cols], cw_ref[2:3, cols]
                            b = cb_ref[:, cols]
                            x_e, x_o, x_em, x_om = rows(0), rows(1), rows(-2), rows(-1)
                            return (b + w0 * x_em + w1 * x_om + w2 * x_e,
                                    b + w0 * x_om + w1 * x_e + w2 * x_o)

                        gate_e, gate_o = conv(g_slab, cwg_ref, cbg_ref)
                        up_e, up_o = conv(u_slab, cwu_ref, cbu_ref)
                        stage_ref[pl.ds(r0, half, stride=2), :] = (
                            gate_e * (1.0 / (1.0 + jnp.exp(-gate_e))) * up_e)
                        stage_ref[pl.ds(r0 + 1, half, stride=2), :] = (
                            gate_o * (1.0 / (1.0 + jnp.exp(-gate_o))) * up_o)
                        h_ref[r0:r0 + ep, cols] = stage_ref[r0:r0 + ep, :].astype(BF16)

    @pl.when(lax.rem(s, 2) == 0)
    def _():
        step(raw0_ref, raw1_ref)

    @pl.when(lax.rem(s, 2) == 1)
    def _():
        step(raw1_ref, raw0_ref)


def _ffn_up(hn, w_gu, conv_w, conv_b, t, layer, bm=1024, bf=512, cw=256, mp=256, ep=128):
    m, dp = hn.shape
    d = w_gu.shape[1]
    dff = w_gu.shape[2] // 2
    ni, nj = m // bm, dff // bf
    n = ni * nj
    assert t % bm == 0 and bf % cw == 0

    def mm_tile(s):
        q = jnp.minimum(s, n - 1)
        return lax.rem(q, ni), lax.div(q, ni)

    def ep_tile(s):
        q = jnp.maximum(s - 1, 0)
        return lax.rem(q, ni), lax.div(q, ni)

    nslab = 2 * bf // LANES
    raw = pltpu.VMEM((nslab, bm + SUBLANES, LANES), F32)
    return pl.pallas_call(
        functools.partial(_ffn_up_kernel, ni=ni, tiles_per_seq=t // bm, cw=cw, mp=mp, ep=ep),
        grid=(n + 1,),
        in_specs=[
            pl.BlockSpec((bm, dp), lambda s: (mm_tile(s)[0], 0)),
            pl.BlockSpec((None, d, bf), lambda s: (layer, 0, mm_tile(s)[1])),
            pl.BlockSpec((None, d, bf), lambda s: (layer, 0, nj + mm_tile(s)[1])),
            pl.BlockSpec((None, CONV_WIDTH, bf), lambda s: (layer, 0, ep_tile(s)[1])),
            pl.BlockSpec((None, CONV_WIDTH, bf), lambda s: (layer, 0, nj + ep_tile(s)[1])),
            pl.BlockSpec((None, 1, bf), lambda s: (layer, 0, ep_tile(s)[1])),
            pl.BlockSpec((None, 1, bf), lambda s: (layer, 0, nj + ep_tile(s)[1])),
        ],
        out_specs=pl.BlockSpec((bm, bf), lambda s: ep_tile(s)),
        out_shape=jax.ShapeDtypeStruct((m, dff), BF16),
        scratch_shapes=[
            pltpu.VMEM((d, 2 * bf), BF16),
            pltpu.VMEM((nslab, SUBLANES, LANES), F32),
            raw, raw,
            pltpu.VMEM((bm, LANES), F32),
        ],
        compiler_params=_params("arbitrary"),
        name="ffn_up",
    )(hn, w_gu, w_gu, conv_w, conv_w, conv_b, conv_b)


def _ffn_down_kernel(x_ref, h_ref, w_ref, o_ref):
    o_ref[...] = x_ref[...] + jnp.dot(h_ref[...], w_ref[...], preferred_element_type=F32)


def _ffn_down(x, h, w, layer, bm=1024, bn=512):
    m, d = x.shape
    k = h.shape[1]
    return pl.pallas_call(
        _ffn_down_kernel,
        grid=(m // bm, d // bn),
        in_specs=[
            pl.BlockSpec((bm, bn), lambda i, j: (i, j)),
            pl.BlockSpec((bm, k), lambda i, j: (i, 0)),
            pl.BlockSpec((None, k, bn), lambda i, j: (layer, 0, j)),
        ],
        out_specs=pl.BlockSpec((bm, bn), lambda i, j: (i, j)),
        out_shape=jax.ShapeDtypeStruct((m, d), F32),
        compiler_params=_params("arbitrary", "arbitrary"),
        name="ffn_down",
    )(x, h, w)


def _rmsnorm_kernel(x_ref, g_ref, o_ref):
    x = x_ref[...]
    y = x * lax.rsqrt(jnp.mean(x * x, axis=-1, keepdims=True) + EPS)
    o_ref[...] = y * g_ref[...]


def _rmsnorm(x, g, bm=1024):
    m, d = x.shape
    return pl.pallas_call(
        _rmsnorm_kernel,
        grid=(m // bm,),
        in_specs=[pl.BlockSpec((bm, d), lambda i: (i, 0)), pl.BlockSpec((1, d), lambda i: (0, 0))],
        out_specs=pl.BlockSpec((bm, d), lambda i: (i, 0)),
        out_shape=jax.ShapeDtypeStruct((m, d), F32),
        compiler_params=_params("arbitrary"),
        name="final_rmsnorm",
    )(x, g)


def _layer(x, nb, t, layer, g_mix, w_qkv, w_f, b_f, g_head, w_o, g_ffn, w_gu, conv_w, conv_b, w_down,
           slopes):
    d = x.shape[1]
    b_pad = jnp.pad(b_f, (0, LANES - N_HEADS_C)).reshape(1, LANES)
    qkv, f = _inproj(x, g_mix.reshape(1, d), w_qkv, w_f, layer)
    ccol, crow = _gate_cumsum(f, b_pad, t)
    o_a = _dilated_attention(qkv, g_head, slopes, nb, t)
    o_b = _stickbreak_attention(qkv, g_head, nb, t)
    o_c = _forgetting_attention(qkv, ccol, crow, g_head, nb, t)
    x1, hn = _outproj(x, o_a, o_b, o_c, w_o, g_ffn.reshape(1, d), layer)
    h = _ffn_up(hn, w_gu, conv_w, conv_b, t, layer)
    return _ffn_down(x1, h, w_down, layer)


def _query_column_scale():
    scale = 1.0 / math.sqrt(HEAD_DIM)
    blk = jnp.arange(N_QKV_BLOCKS)
    is_qb = (blk >= QB0) & (blk < KB0)
    is_q = (blk < KA0) | is_qb | ((blk >= QC0) & (blk < KC0))
    per_block = jnp.where(is_qb, scale * LOG2_E, jnp.where(is_q, scale, 1.0)).astype(F32)
    return jnp.repeat(per_block, HEAD_DIM).reshape(1, -1)


def kernel(x, g_mix, w_in, b_f, g_head, w_o, g_ffn, w_gu, conv_w, conv_b, w_down, g_final):
    nb, t, d = x.shape
    depth = g_mix.shape[0]
    n_qkv = N_QKV_BLOCKS * HEAD_DIM
    slopes = 2.0 ** (-8.0 * (jnp.arange(N_HEADS_A, dtype=F32) + 1.0) / N_HEADS_A)
    w_qkv = _cast_weights(w_in, _query_column_scale(), bk=256)
    w_f = jnp.pad(w_in[:, :, n_qkv:], ((0, 0), (0, 0), (0, LANES - N_HEADS_C))).astype(BF16)
    w_o16 = _cast_weights(w_o, jnp.ones((1, d), F32), bk=1024)
    w_down16 = _cast_weights(w_down, jnp.ones((1, d), F32), bk=w_down.shape[1] // 8)
    conv_b3 = conv_b.reshape(depth, 1, -1)
    xf = x.reshape(nb * t, d)
    for layer in range(depth):
        xf = _layer(xf, nb, t, layer, g_mix[layer], w_qkv, w_f[layer], b_f[layer], g_head[layer], w_o16,
                    g_ffn[layer], w_gu, conv_w, conv_b3, w_down16, slopes)
    return _rmsnorm(xf, g_final.reshape(1, d)).reshape(nb, t, d)
```

```python
import functools
import math

import jax
import jax.numpy as jnp
from jax import lax
from jax.experimental import pallas as pl
from jax.experimental.pallas import tpu as pltpu

F32 = jnp.float32
BF16 = jnp.bfloat16

HEAD_DIM = 128
N_HEADS_A = 6
N_HEADS_B = 5
N_HEADS_C = 5
N_HEADS = N_HEADS_A + N_HEADS_B + N_HEADS_C
DILATION_PATTERNS = ((128, 1), (512, 4), (2048, 16))
WINDOW_BLOCK = 128
CONV_WIDTH = 3
EPS = 1e-6
NEG = -1e30
LANES = 128
SUBLANES = 8
VMEM_LIMIT_BYTES = 56 * 1024 * 1024
LHS_PAD = LANES
LOG2_E = 1.4426950408889634

QA0, KA0, VA0 = 0, N_HEADS_A, 2 * N_HEADS_A
QB0 = 3 * N_HEADS_A
KB0, VB0 = QB0 + N_HEADS_B, QB0 + 2 * N_HEADS_B
QC0 = QB0 + 3 * N_HEADS_B
KC0, VC0 = QC0 + N_HEADS_C, QC0 + 2 * N_HEADS_C
N_QKV_BLOCKS = QC0 + 3 * N_HEADS_C

ATT_BQ = 256
ATT_CHUNK = 256
FOX_CHUNK = 512


def _params(*sem):
    return pltpu.CompilerParams(dimension_semantics=sem, vmem_limit_bytes=VMEM_LIMIT_BYTES)


def _nt_dot(a, b):
    return lax.dot_general(a, b, (((1,), (1,)), ((), ())), preferred_element_type=F32)


def _head_rmsnorm(o, g):
    return o * lax.rsqrt(jnp.mean(o * o, axis=-1, keepdims=True) + EPS) * g


def _log_sigmoid(x):
    return jnp.minimum(x, 0.0) - jnp.log(1.0 + jnp.exp(-jnp.abs(x)))


def _cast_kernel(w_ref, s_ref, o_ref):
    o_ref[...] = (w_ref[...] * s_ref[...]).astype(BF16)


def _cast_weights(w, col_scale, bk):
    nl, k, _ = w.shape
    n = col_scale.shape[1]
    return pl.pallas_call(
        _cast_kernel,
        grid=(nl, k // bk),
        in_specs=[
            pl.BlockSpec((None, bk, n), lambda l, i: (l, i, 0)),
            pl.BlockSpec((1, n), lambda l, i: (0, 0)),
        ],
        out_specs=pl.BlockSpec((None, bk, n), lambda l, i: (l, i, 0)),
        out_shape=jax.ShapeDtypeStruct((nl, k, n), BF16),
        compiler_params=_params("arbitrary", "arbitrary"),
        name="cast_weights",
    )(w, col_scale)


def _inproj_kernel(x_ref, g_ref, w_ref, wf_ref, qkv_ref, f_ref, hn_ref):
    d = x_ref.shape[1]

    @pl.when(pl.program_id(1) == 0)
    def _():
        x = x_ref[...]
        y = x * lax.rsqrt(jnp.mean(x * x, axis=-1, keepdims=True) + EPS)
        hn_ref[:, 0:d] = (y * g_ref[...]).astype(BF16)
        f_ref[...] = jnp.dot(hn_ref[:, 0:d], wf_ref[...], preferred_element_type=F32)

    acc = jnp.dot(hn_ref[:, 0:d], w_ref[...], preferred_element_type=F32)
    for c in range(acc.shape[1] // LANES):
        qkv_ref[c] = acc[:, c * LANES:(c + 1) * LANES].astype(BF16)


def _inproj(x, g, w, wf, layer, bm=1024, bn=1024):
    m, d = x.shape
    n = w.shape[2]
    return pl.pallas_call(
        _inproj_kernel,
        grid=(m // bm, n // bn),
        in_specs=[
            pl.BlockSpec((bm, d), lambda i, j: (i, 0)),
            pl.BlockSpec((1, d), lambda i, j: (0, 0)),
            pl.BlockSpec((None, d, bn), lambda i, j: (layer, 0, j)),
            pl.BlockSpec((d, LANES), lambda i, j: (0, 0)),
        ],
        out_specs=[
            pl.BlockSpec((bn // LANES, bm, LANES), lambda i, j: (j, i, 0)),
            pl.BlockSpec((bm, LANES), lambda i, j: (i, 0)),
        ],
        out_shape=[
            jax.ShapeDtypeStruct((n // LANES, m, LANES), BF16),
            jax.ShapeDtypeStruct((m, LANES), F32),
        ],
        scratch_shapes=[pltpu.VMEM((bm, d + LHS_PAD), BF16)],
        compiler_params=_params("arbitrary", "arbitrary"),
        name="inproj",
    )(x, g, w, wf)


def _gate_cumsum_kernel(f_ref, b_ref, ccol_ref, crow_ref):
    t = f_ref.shape[0]
    blk = LANES
    r_i = lax.broadcasted_iota(jnp.int32, (blk, blk), 0)
    c_i = lax.broadcasted_iota(jnp.int32, (blk, blk), 1)
    tri = jnp.where(c_i <= r_i, 1.0, 0.0).astype(BF16)
    carry = jnp.zeros((1, LANES), F32)
    for n in range(t // blk):
        lf = _log_sigmoid(f_ref[n * blk:(n + 1) * blk, :] + b_ref[...])
        h1 = lf.astype(BF16)
        r1 = lf - h1.astype(F32)
        h2 = r1.astype(BF16)
        h3 = (r1 - h2.astype(F32)).astype(BF16)
        c = (jnp.dot(tri, h1, preferred_element_type=F32)
             + jnp.dot(tri, h2, preferred_element_type=F32)
             + jnp.dot(tri, h3, preferred_element_type=F32)) + carry
        ccol_ref[n * blk:(n + 1) * blk, :] = c
        crow_ref[:, n * blk:(n + 1) * blk] = jnp.transpose(c)[0:SUBLANES, :]
        carry = c[blk - 1:blk, :]


def _gate_cumsum(f, b_pad, t):
    m = f.shape[0]
    nb = m // t
    return pl.pallas_call(
        _gate_cumsum_kernel,
        grid=(nb,),
        in_specs=[
            pl.BlockSpec((t, LANES), lambda b: (b, 0)),
            pl.BlockSpec((1, LANES), lambda b: (0, 0)),
        ],
        out_specs=[
            pl.BlockSpec((t, LANES), lambda b: (b, 0)),
            pl.BlockSpec((None, SUBLANES, t), lambda b: (b, 0, 0)),
        ],
        out_shape=[
            jax.ShapeDtypeStruct((m, LANES), F32),
            jax.ShapeDtypeStruct((nb, SUBLANES, t), F32),
        ],
        compiler_params=_params("arbitrary"),
        name="gate_cumsum",
    )(f, b_pad)


def _dilated_kernel(slopes_ref, q_ref, k_ref, v_ref, g_ref, o_ref,
                    qf_ref, kf_ref, vf_ref, qg_ref, kg_ref, vg_ref, qd_ref, kd_ref, va_ref, s_ref, p_ref,
                    mx_ref, oacc_ref, lse_ref):
    h = pl.program_id(1)
    t = q_ref.shape[0]
    wb = WINDOW_BLOCK
    slope = slopes_ref[h]
    qf_ref[...] = q_ref[...].astype(F32)
    kf_ref[...] = k_ref[...].astype(F32)
    vf_ref[...] = v_ref[...].astype(F32)
    for p in range(len(DILATION_PATTERNS)):
        kd_ref[p, 0:wb, :] = jnp.zeros((wb, HEAD_DIM), BF16)
        va_ref[p, 0:wb, :] = jnp.zeros((wb, 2 * HEAD_DIM), BF16)
        va_ref[p, wb:, HEAD_DIM:] = jnp.ones((t, HEAD_DIM), BF16)

    q_i = lax.broadcasted_iota(jnp.int32, (wb, 2 * wb), 0)
    k_i = lax.broadcasted_iota(jnp.int32, (wb, 2 * wb), 1)
    delta_i = wb + q_i - k_i
    delta = delta_i.astype(F32)
    in_window = (delta_i >= 0) & (delta_i <= wb)
    rc = 256
    prev_dil, (qsrc, ksrc, vsrc) = 1, (qf_ref, kf_ref, vf_ref)

    for p, (window, dil) in enumerate(DILATION_PATTERNS):
        assert window // dil == wb and t % (dil * wb) == 0
        seq = t // dil
        nblk = seq // wb
        has_prev = nblk > 1
        bias_band = jnp.where(in_window, (-slope * dil) * delta, NEG)
        bias_first = jnp.where(k_i >= wb, bias_band, NEG)
        if dil == 1:
            qd_ref[p] = q_ref[...]
            kd_ref[p, wb:, :] = k_ref[...]
            va_ref[p, wb:, 0:HEAD_DIM] = v_ref[...]
        else:
            step = dil // prev_dil
            assert prev_dil * step == dil
            keep = p + 1 < len(DILATION_PATTERNS)
            for src_ref, keep_ref, dst_ref, row0, lanes in (
                    (qsrc, qg_ref, qd_ref, 0, slice(None)),
                    (ksrc, kg_ref, kd_ref, wb, slice(None)),
                    (vsrc, vg_ref, va_ref, wb, slice(0, HEAD_DIM))):
                for r1 in range(prev_dil):
                    for r2 in range(step):
                        r = r1 + prev_dil * r2
                        x = src_ref[pl.ds(r1 * (t // prev_dil) + r2, seq, stride=step), :]
                        if keep:
                            keep_ref[r * seq:(r + 1) * seq, :] = x
                        dst_ref[p, row0 + r * seq:row0 + (r + 1) * seq, lanes] = x.astype(BF16)
            if keep:
                qsrc, ksrc, vsrc = qg_ref, kg_ref, vg_ref
        prev_dil = dil
        width = 2 * wb if has_prev else wb
        for blk in range(t // wb):
            rows = slice(blk * wb, (blk + 1) * wb)
            if has_prev:
                band = kd_ref[p, blk * wb:(blk + 2) * wb, :]
                bias = bias_first if blk % nblk == 0 else bias_band
            else:
                band = kd_ref[p, (blk + 1) * wb:(blk + 2) * wb, :]
                bias = bias_band[:, wb:]
            s_ref[rows, 0:width] = _nt_dot(qd_ref[p, rows, :], band) + bias
        for c in range(t // rc):
            rows = slice(c * rc, (c + 1) * rc)
            s = s_ref[rows, 0:width]
            mx = jnp.max(s, axis=-1, keepdims=True)
            p_ref[p, rows, 0:width] = jnp.exp(s - mx).astype(BF16)
            mx_ref[rows, :] = jnp.broadcast_to(mx, (rc, LANES))
        for blk in range(t // wb):
            rows = slice(blk * wb, (blk + 1) * wb)
            vrows = slice(blk * wb, (blk + 2) * wb) if has_prev else slice((blk + 1) * wb, (blk + 2) * wb)
            o2 = jnp.dot(p_ref[p, rows, 0:width], va_ref[p, vrows, :], preferred_element_type=F32)
            l = o2[:, HEAD_DIM:]
            r, n = divmod(blk, nblk)
            dest = pl.ds(r + dil * wb * n, wb, stride=dil) if dil > 1 else pl.ds(blk * wb, wb)
            oacc_ref[p, dest, :] = o2[:, 0:HEAD_DIM] / l
            lse_ref[p, dest, :] = mx_ref[rows, :] + jnp.log(l)

    g = g_ref[pl.ds(h, 1), :]
    for c in range(t // rc):
        rs = slice(c * rc, (c + 1) * rc)
        l0, l1, l2 = lse_ref[0, rs, :], lse_ref[1, rs, :], lse_ref[2, rs, :]
        mx = jnp.maximum(jnp.maximum(l0, l1), l2)
        e0, e1, e2 = jnp.exp(l0 - mx), jnp.exp(l1 - mx), jnp.exp(l2 - mx)
        o = (e0 * oacc_ref[0, rs, :] + e1 * oacc_ref[1, rs, :] + e2 * oacc_ref[2, rs, :]) / (e0 + e1 + e2)
        o_ref[rs, :] = _head_rmsnorm(o, g).astype(BF16)


def _dilated_attention(qkv, g_head, slopes, nb, t):
    m = nb * t
    npat = len(DILATION_PATTERNS)
    wb = WINDOW_BLOCK
    grid_spec = pltpu.PrefetchScalarGridSpec(
        num_scalar_prefetch=1,
        grid=(nb, N_HEADS_A),
        in_specs=[
            pl.BlockSpec((None, t, HEAD_DIM), lambda b, h, s: (QA0 + h, b, 0)),
            pl.BlockSpec((None, t, HEAD_DIM), lambda b, h, s: (KA0 + h, b, 0)),
            pl.BlockSpec((None, t, HEAD_DIM), lambda b, h, s: (VA0 + h, b, 0)),
            pl.BlockSpec((N_HEADS, HEAD_DIM), lambda b, h, s: (0, 0)),
        ],
        out_specs=pl.BlockSpec((t, HEAD_DIM), lambda b, h, s: (b, h)),
        scratch_shapes=[
            pltpu.VMEM((t, HEAD_DIM), F32),
            pltpu.VMEM((t, HEAD_DIM), F32),
            pltpu.VMEM((t, HEAD_DIM), F32),
            pltpu.VMEM((t, HEAD_DIM), F32),
            pltpu.VMEM((t, HEAD_DIM), F32),
            pltpu.VMEM((t, HEAD_DIM), F32),
            pltpu.VMEM((npat, t, HEAD_DIM), BF16),
            pltpu.VMEM((npat, t + wb, HEAD_DIM), BF16),
            pltpu.VMEM((npat, t + wb, 2 * HEAD_DIM), BF16),
            pltpu.VMEM((t, 2 * wb), F32),
            pltpu.VMEM((npat, t, 2 * wb), BF16),
            pltpu.VMEM((t, LANES), F32),
            pltpu.VMEM((npat, t, HEAD_DIM), F32),
            pltpu.VMEM((npat, t, LANES), F32),
        ],
    )
    return pl.pallas_call(
        _dilated_kernel,
        grid_spec=grid_spec,
        out_shape=jax.ShapeDtypeStruct((m, N_HEADS_A * HEAD_DIM), BF16),
        compiler_params=_params("arbitrary", "arbitrary"),
        name="dilated_attention",
    )(slopes, qkv, qkv, qkv, g_head)


def _stickbreak_kernel(q_ref, k_ref, v_ref, g_ref, o_ref, a_ref, lsig_ref, l1m_ref):
    j = pl.program_id(1)
    t = q_ref.shape[0]
    bq, ck = ATT_BQ, ATT_CHUNK
    assert bq == ck
    g = g_ref[pl.ds(N_HEADS_A + j, 1), :]
    r_i = lax.broadcasted_iota(jnp.int32, (ck, ck), 0)
    c_i = lax.broadcasted_iota(jnp.int32, (ck, ck), 1)
    suffix = jnp.where(r_i > c_i, 1.0, 0.0).astype(BF16)
    causal = c_i < r_i
    for n in range(t // bq):
        q = q_ref[n * bq:(n + 1) * bq, :]
        pair0 = n * (n + 1) // 2
        row_sum = {}
        for c in range(n + 1):
            z2 = _nt_dot(q, k_ref[c * ck:(c + 1) * ck, :])
            log_1m = jnp.minimum(-z2, 0.0) - jnp.log2(1.0 + jnp.exp2(-jnp.abs(z2)))
            lsig_ref[:, c * ck:(c + 1) * ck] = z2 + log_1m
            if c == n:
                log_1m = jnp.where(causal, log_1m, 0.0)
            l1m_ref[pair0 + c] = log_1m.astype(BF16)
            if c > 0:
                row_sum[c] = jnp.sum(log_1m, axis=-1, keepdims=True)
        later = jnp.zeros((bq, 1), F32)
        for c in range(n, -1, -1):
            after = jnp.dot(l1m_ref[pair0 + c], suffix, preferred_element_type=F32) + later
            a = jnp.exp2(lsig_ref[:, c * ck:(c + 1) * ck] + after)
            if c == n:
                a = jnp.where(causal, a, 0.0)
            a_ref[n, :, c * ck:(c + 1) * ck] = a.astype(BF16)
            if c > 0:
                later = later + row_sum[c]
        ext = (n + 1) * bq
        o = jnp.dot(a_ref[n, :, 0:ext], v_ref[0:ext, :], preferred_element_type=F32)
        o_ref[n * bq:(n + 1) * bq, :] = _head_rmsnorm(o, g).astype(BF16)


def _stickbreak_attention(qkv, g_head, nb, t):
    m = nb * t
    return pl.pallas_call(
        _stickbreak_kernel,
        grid=(nb, N_HEADS_B),
        in_specs=[
            pl.BlockSpec((None, t, HEAD_DIM), lambda b, j: (QB0 + j, b, 0)),
            pl.BlockSpec((None, t, HEAD_DIM), lambda b, j: (KB0 + j, b, 0)),
            pl.BlockSpec((None, t, HEAD_DIM), lambda b, j: (VB0 + j, b, 0)),
            pl.BlockSpec((N_HEADS, HEAD_DIM), lambda b, j: (0, 0)),
        ],
        out_specs=pl.BlockSpec((t, HEAD_DIM), lambda b, j: (b, j)),
        out_shape=jax.ShapeDtypeStruct((m, N_HEADS_B * HEAD_DIM), BF16),
        scratch_shapes=[
            pltpu.VMEM((t // ATT_BQ, ATT_BQ, t), BF16),
            pltpu.VMEM((ATT_BQ, t), F32),
            pltpu.VMEM(((t // ATT_BQ) * (t // ATT_BQ + 1) // 2, ATT_BQ, ATT_CHUNK), BF16),
        ],
        compiler_params=_params("arbitrary", "arbitrary"),
        name="stickbreak_attention",
    )(qkv, qkv, qkv, g_head)


def _forgetting_kernel(q_ref, k_ref, v_ref, ccol_ref, crow_ref, g_ref, o_ref, s_ref, p_ref):
    j = pl.program_id(1)
    t = q_ref.shape[0]
    bq, ck = ATT_BQ, FOX_CHUNK
    g = g_ref[pl.ds(N_HEADS_A + N_HEADS_B + j, 1), :]
    lane = lax.broadcasted_iota(jnp.int32, (1, LANES), 1)
    r_i = lax.broadcasted_iota(jnp.int32, (bq, bq), 0)
    c_i = lax.broadcasted_iota(jnp.int32, (bq, bq), 1)
    causal = c_i <= r_i
    for n in range(t // bq):
        q0, ext = n * bq, (n + 1) * bq
        q = q_ref[q0:ext, :]
        c_t = jnp.sum(jnp.where(lane == j, ccol_ref[q0:ext, :], 0.0), axis=-1, keepdims=True)
        mx = None
        for k0 in range(0, ext, ck):
            k1 = min(k0 + ck, ext)
            s = _nt_dot(q, k_ref[k0:k1, :]) + c_t - crow_ref[pl.ds(j, 1), k0:k1]
            if k1 == ext:
                split = k1 - k0 - bq
                diag = jnp.where(causal, s[:, split:], NEG)
                s = diag if split == 0 else jnp.concatenate([s[:, :split], diag], axis=1)
            s_ref[:, k0:k1] = s
            cm = jnp.max(s, axis=-1, keepdims=True)
            mx = cm if mx is None else jnp.maximum(mx, cm)
        l = jnp.zeros((bq, 1), F32)
        for k0 in range(0, ext, ck):
            k1 = min(k0 + ck, ext)
            p = jnp.exp(s_ref[:, k0:k1] - mx)
            l = l + jnp.sum(p, axis=-1, keepdims=True)
            p_ref[n, :, k0:k1] = p.astype(BF16)
        o = jnp.dot(p_ref[n, :, 0:ext], v_ref[0:ext, :], preferred_element_type=F32) / l
        o_ref[q0:ext, :] = _head_rmsnorm(o, g).astype(BF16)


def _forgetting_attention(qkv, ccol, crow, g_head, nb, t):
    m = nb * t
    return pl.pallas_call(
        _forgetting_kernel,
        grid=(nb, N_HEADS_C),
        in_specs=[
            pl.BlockSpec((None, t, HEAD_DIM), lambda b, j: (QC0 + j, b, 0)),
            pl.BlockSpec((None, t, HEAD_DIM), lambda b, j: (KC0 + j, b, 0)),
            pl.BlockSpec((None, t, HEAD_DIM), lambda b, j: (VC0 + j, b, 0)),
            pl.BlockSpec((t, LANES), lambda b, j: (b, 0)),
            pl.BlockSpec((None, SUBLANES, t), lambda b, j: (b, 0, 0)),
            pl.BlockSpec((N_HEADS, HEAD_DIM), lambda b, j: (0, 0)),
        ],
        out_specs=pl.BlockSpec((t, HEAD_DIM), lambda b, j: (b, j)),
        out_shape=jax.ShapeDtypeStruct((m, N_HEADS_C * HEAD_DIM), BF16),
        scratch_shapes=[pltpu.VMEM((ATT_BQ, t), F32), pltpu.VMEM((t // ATT_BQ, ATT_BQ, t), BF16)],
        compiler_params=_params("arbitrary", "arbitrary"),
        name="forgetting_attention",
    )(qkv, qkv, qkv, ccol, crow, g_head)


def _outproj_kernel(x_ref, oa_ref, ob_ref, oc_ref, w_ref, g_ref, x1_ref, hn_ref):
    ka = oa_ref.shape[1]
    kb = ka + ob_ref.shape[1]
    x1 = (x_ref[...]
          + jnp.dot(oa_ref[...], w_ref[0:ka, :], preferred_element_type=F32)
          + jnp.dot(ob_ref[...], w_ref[ka:kb, :], preferred_element_type=F32)
          + jnp.dot(oc_ref[...], w_ref[kb:, :], preferred_element_type=F32))
    x1_ref[...] = x1
    d = x1.shape[1]
    y = x1 * lax.rsqrt(jnp.mean(x1 * x1, axis=-1, keepdims=True) + EPS)
    hn_ref[:, 0:d] = (y * g_ref[...]).astype(BF16)
    hn_ref[:, d:] = jnp.zeros((x1.shape[0], hn_ref.shape[1] - d), BF16)


def _outproj(x, oa, ob, oc, w, g, layer, bm=512):
    m, d = x.shape
    row = lambda i: (i, 0)
    whole = lambda i: (0, 0)
    return pl.pallas_call(
        _outproj_kernel,
        grid=(m // bm,),
        in_specs=[
            pl.BlockSpec((bm, d), row),
            pl.BlockSpec((bm, oa.shape[1]), row),
            pl.BlockSpec((bm, ob.shape[1]), row),
            pl.BlockSpec((bm, oc.shape[1]), row),
            pl.BlockSpec((None, d, d), lambda i: (layer, 0, 0)),
            pl.BlockSpec((1, d), whole),
        ],
        out_specs=[pl.BlockSpec((bm, d), row), pl.BlockSpec((bm, d + LHS_PAD), row)],
        out_shape=[jax.ShapeDtypeStruct((m, d), F32), jax.ShapeDtypeStruct((m, d + LHS_PAD), BF16)],
        compiler_params=_params("arbitrary"),
        name="outproj",
    )(x, oa, ob, oc, w, g)


def _ffn_up_kernel(hn_ref, wg_ref, wu_ref, cwg_ref, cwu_ref, cbg_ref, cbu_ref, h_ref,
                   w16_ref, halo_ref, raw0_ref, raw1_ref, stage_ref, *, ni, tiles_per_seq, cw, mp, ep):
    s = pl.program_id(0)
    last = pl.num_programs(0) - 2
    i = lax.rem(jnp.minimum(s, last), ni)
    ip = lax.rem(jnp.maximum(s - 1, 0), ni)
    bm, (d, bf) = hn_ref.shape[0], wg_ref.shape
    pad = SUBLANES
    spc = cw // LANES
    half = ep // 2

    @pl.when(i == 0)
    def _():
        for c in range(bf // cw):
            w16_ref[:, 2 * c * cw:(2 * c + 1) * cw] = wg_ref[:, c * cw:(c + 1) * cw].astype(BF16)
            w16_ref[:, (2 * c + 1) * cw:(2 * c + 2) * cw] = wu_ref[:, c * cw:(c + 1) * cw].astype(BF16)

    @pl.when(s == 0)
    def _():
        raw1_ref[...] = jnp.zeros(raw1_ref.shape, F32)

    @pl.when(lax.rem(ip, tiles_per_seq) == 0)
    def _():
        halo_ref[...] = jnp.zeros(halo_ref.shape, F32)

    def step(cur_ref, prev_ref):
        prev_ref[:, 0:pad, :] = halo_ref[...]
        halo_ref[...] = prev_ref[:, bm:bm + pad, :]
        for c in range(bf // cw):
            for m0 in range(0, bm, mp):
                res = jnp.dot(hn_ref[m0:m0 + mp, 0:d], w16_ref[:, 2 * c * cw:(2 * c + 2) * cw],
                              preferred_element_type=F32)
                for k in range(2 * spc):
                    cur_ref[2 * c * spc + k, pad + m0:pad + m0 + mp, :] = res[:, k * LANES:(k + 1) * LANES]
                for k in range(spc):
                    g_slab, u_slab = 2 * c * spc + k, (2 * c + 1) * spc + k
                    cols = slice(c * cw + k * LANES, c * cw + (k + 1) * LANES)
                    for r0 in range(m0, m0 + mp, ep):

                        def conv(slab, cw_ref, cb_ref):
                            def rows(first):
                                return prev_ref[slab, pl.ds(pad + r0 + first, half, stride=2), :]
                            w0, w1, w2 = cw_ref[0:1, cols], cw_ref[1:2, cols], cw_ref[2:3, cols]
                            b = cb_ref[:, cols]
                            x_e, x_o, x_em, x_om = rows(0), rows(1), rows(-2), rows(-1)
                            return (b + w0 * x_em + w1 * x_om + w2 * x_e,
                                    b + w0 * x_om + w1 * x_e + w2 * x_o)

                        gate_e, gate_o = conv(g_slab, cwg_ref, cbg_ref)
                        up_e, up_o = conv(u_slab, cwu_ref, cbu_ref)
                        stage_ref[pl.ds(r0, half, stride=2), :] = (
                            gate_e * (1.0 / (1.0 + jnp.exp(-gate_e))) * up_e)
                        stage_ref[pl.ds(r0 + 1, half, stride=2), :] = (
                            gate_o * (1.0 / (1.0 + jnp.exp(-gate_o))) * up_o)
                        h_ref[r0:r0 + ep, cols] = stage_ref[r0:r0 + ep, :].astype(BF16)

    @pl.when(lax.rem(s, 2) == 0)
    def _():
        step(raw0_ref, raw1_ref)

    @pl.when(lax.rem(s, 2) == 1)
    def _():
        step(raw1_ref, raw0_ref)


def _ffn_up(hn, w_gu, conv_w, conv_b, t, layer, bm=1024, bf=512, cw=256, mp=256, ep=128):
    m, dp = hn.shape
    d = w_gu.shape[1]
    dff = w_gu.shape[2] // 2
    ni, nj = m // bm, dff // bf
    n = ni * nj
    assert t % bm == 0 and bf % cw == 0

    def mm_tile(s):
        q = jnp.minimum(s, n - 1)
        return lax.rem(q, ni), lax.div(q, ni)

    def ep_tile(s):
        q = jnp.maximum(s - 1, 0)
        return lax.rem(q, ni), lax.div(q, ni)

    nslab = 2 * bf // LANES
    raw = pltpu.VMEM((nslab, bm + SUBLANES, LANES), F32)
    return pl.pallas_call(
        functools.partial(_ffn_up_kernel, ni=ni, tiles_per_seq=t // bm, cw=cw, mp=mp, ep=ep),
        grid=(n + 1,),
        in_specs=[
            pl.BlockSpec((bm, dp), lambda s: (mm_tile(s)[0], 0)),
            pl.BlockSpec((None, d, bf), lambda s: (layer, 0, mm_tile(s)[1])),
            pl.BlockSpec((None, d, bf), lambda s: (layer, 0, nj + mm_tile(s)[1])),
            pl.BlockSpec((None, CONV_WIDTH, bf), lambda s: (layer, 0, ep_tile(s)[1])),
            pl.BlockSpec((None, CONV_WIDTH, bf), lambda s: (layer, 0, nj + ep_tile(s)[1])),
            pl.BlockSpec((None, 1, bf), lambda s: (layer, 0, ep_tile(s)[1])),
            pl.BlockSpec((None, 1, bf), lambda s: (layer, 0, nj + ep_tile(s)[1])),
        ],
        out_specs=pl.BlockSpec((bm, bf), lambda s: ep_tile(s)),
        out_shape=jax.ShapeDtypeStruct((m, dff), BF16),
        scratch_shapes=[
            pltpu.VMEM((d, 2 * bf), BF16),
            pltpu.VMEM((nslab, SUBLANES, LANES), F32),
            raw, raw,
            pltpu.VMEM((bm, LANES), F32),
        ],
        compiler_params=_params("arbitrary"),
        name="ffn_up",
    )(hn, w_gu, w_gu, conv_w, conv_w, conv_b, conv_b)


def _ffn_down_kernel(x_ref, h_ref, w_ref, o_ref):
    o_ref[...] = x_ref[...] + jnp.dot(h_ref[...], w_ref[...], preferred_element_type=F32)


def _ffn_down(x, h, w, layer, bm=1024, bn=512):
    m, d = x.shape
    k = h.shape[1]
    return pl.pallas_call(
        _ffn_down_kernel,
        grid=(m // bm, d // bn),
        in_specs=[
            pl.BlockSpec((bm, bn), lambda i, j: (i, j)),
            pl.BlockSpec((bm, k), lambda i, j: (i, 0)),
            pl.BlockSpec((None, k, bn), lambda i, j: (layer, 0, j)),
        ],
        out_specs=pl.BlockSpec((bm, bn), lambda i, j: (i, j)),
        out_shape=jax.ShapeDtypeStruct((m, d), F32),
        compiler_params=_params("arbitrary", "arbitrary"),
        name="ffn_down",
    )(x, h, w)


def _rmsnorm_kernel(x_ref, g_ref, o_ref):
    x = x_ref[...]
    y = x * lax.rsqrt(jnp.mean(x * x, axis=-1, keepdims=True) + EPS)
    o_ref[...] = y * g_ref[...]


def _rmsnorm(x, g, bm=1024):
    m, d = x.shape
    return pl.pallas_call(
        _rmsnorm_kernel,
        grid=(m // bm,),
        in_specs=[pl.BlockSpec((bm, d), lambda i: (i, 0)), pl.BlockSpec((1, d), lambda i: (0, 0))],
        out_specs=pl.BlockSpec((bm, d), lambda i: (i, 0)),
        out_shape=jax.ShapeDtypeStruct((m, d), F32),
        compiler_params=_params("arbitrary"),
        name="final_rmsnorm",
    )(x, g)


def _layer(x, nb, t, layer, g_mix, w_qkv, w_f, b_f, g_head, w_o, g_ffn, w_gu, conv_w, conv_b, w_down,
           slopes):
    d = x.shape[1]
    b_pad = jnp.pad(b_f, (0, LANES - N_HEADS_C)).reshape(1, LANES)
    qkv, f = _inproj(x, g_mix.reshape(1, d), w_qkv, w_f, layer)
    ccol, crow = _gate_cumsum(f, b_pad, t)
    o_a = _dilated_attention(qkv, g_head, slopes, nb, t)
    o_b = _stickbreak_attention(qkv, g_head, nb, t)
    o_c = _forgetting_attention(qkv, ccol, crow, g_head, nb, t)
    x1, hn = _outproj(x, o_a, o_b, o_c, w_o, g_ffn.reshape(1, d), layer)
    h = _ffn_up(hn, w_gu, conv_w, conv_b, t, layer)
    return _ffn_down(x1, h, w_down, layer)


def _query_column_scale():
    scale = 1.0 / math.sqrt(HEAD_DIM)
    blk = jnp.arange(N_QKV_BLOCKS)
    is_qb = (blk >= QB0) & (blk < KB0)
    is_q = (blk < KA0) | is_qb | ((blk >= QC0) & (blk < KC0))
    per_block = jnp.where(is_qb, scale * LOG2_E, jnp.where(is_q, scale, 1.0)).astype(F32)
    return jnp.repeat(per_block, HEAD_DIM).reshape(1, -1)


def kernel(x, g_mix, w_in, b_f, g_head, w_o, g_ffn, w_gu, conv_w, conv_b, w_down, g_final):
    nb, t, d = x.shape
    depth = g_mix.shape[0]
    n_qkv = N_QKV_BLOCKS * HEAD_DIM
    slopes = 2.0 ** (-8.0 * (jnp.arange(N_HEADS_A, dtype=F32) + 1.0) / N_HEADS_A)
    w_qkv = _cast_weights(w_in, _query_column_scale(), bk=256)
    w_f = jnp.pad(w_in[:, :, n_qkv:], ((0, 0), (0, 0), (0, LANES - N_HEADS_C))).astype(BF16)
    w_o16 = _cast_weights(w_o, jnp.ones((1, d), F32), bk=1024)
    w_down16 = _cast_weights(w_down, jnp.ones((1, d), F32), bk=w_down.shape[1] // 8)
    conv_b3 = conv_b.reshape(depth, 1, -1)
    xf = x.reshape(nb * t, d)
    for layer in range(depth):
        xf = _layer(xf, nb, t, layer, g_mix[layer], w_qkv, w_f[layer], b_f[layer], g_head[layer], w_o16,
                    g_ffn[layer], w_gu, conv_w, conv_b3, w_down16, slopes)
    return _rmsnorm(xf, g_final.reshape(1, d)).reshape(nb, t, d)
```

```python
import functools
import math

import jax
import jax.numpy as jnp
from jax import lax
from jax.experimental import pallas as pl
from jax.experimental.pallas import tpu as pltpu

F32 = jnp.float32
BF16 = jnp.bfloat16

HEAD_DIM = 128
N_HEADS_A = 6
N_HEADS_B = 5
N_HEADS_C = 5
N_HEADS = N_HEADS_A + N_HEADS_B + N_HEADS_C
DILATION_PATTERNS = ((128, 1), (512, 4), (2048, 16))
WINDOW_BLOCK = 128
CONV_WIDTH = 3
EPS = 1e-6
NEG = -1e30
LANES = 128
SUBLANES = 8
VMEM_LIMIT_BYTES = 56 * 1024 * 1024
LHS_PAD = LANES
LOG2_E = 1.4426950408889634

QA0, KA0, VA0 = 0, N_HEADS_A, 2 * N_HEADS_A
QB0 = 3 * N_HEADS_A
KB0, VB0 = QB0 + N_HEADS_B, QB0 + 2 * N_HEADS_B
QC0 = QB0 + 3 * N_HEADS_B
KC0, VC0 = QC0 + N_HEADS_C, QC0 + 2 * N_HEADS_C
N_QKV_BLOCKS = QC0 + 3 * N_HEADS_C

ATT_BQ = 256
ATT_CHUNK = 256
FOX_CHUNK = 512


def _params(*sem):
    return pltpu.CompilerParams(dimension_semantics=sem, vmem_limit_bytes=VMEM_LIMIT_BYTES)


def _nt_dot(a, b):
    return lax.dot_general(a, b, (((1,), (1,)), ((), ())), preferred_element_type=F32)


def _head_rmsnorm(o, g):
    return o * lax.rsqrt(jnp.mean(o * o, axis=-1, keepdims=True) + EPS) * g


def _log_sigmoid(x):
    return jnp.minimum(x, 0.0) - jnp.log(1.0 + jnp.exp(-jnp.abs(x)))


def _cast_kernel(w_ref, s_ref, o_ref):
    o_ref[...] = (w_ref[...] * s_ref[...]).astype(BF16)


def _cast_weights(w, col_scale, bk):
    nl, k, _ = w.shape
    n = col_scale.shape[1]
    return pl.pallas_call(
        _cast_kernel,
        grid=(nl, k // bk),
        in_specs=[
            pl.BlockSpec((None, bk, n), lambda l, i: (l, i, 0)),
            pl.BlockSpec((1, n), lambda l, i: (0, 0)),
        ],
        out_specs=pl.BlockSpec((None, bk, n), lambda l, i: (l, i, 0)),
        out_shape=jax.ShapeDtypeStruct((nl, k, n), BF16),
        compiler_params=_params("arbitrary", "arbitrary"),
        name="cast_weights",
    )(w, col_scale)


def _inproj_kernel(x_ref, g_ref, w_ref, wf_ref, qkv_ref, f_ref, hn_ref):
    d = x_ref.shape[1]

    @pl.when(pl.program_id(1) == 0)
    def _():
        x = x_ref[...]
        y = x * lax.rsqrt(jnp.mean(x * x, axis=-1, keepdims=True) + EPS)
        hn_ref[:, 0:d] = (y * g_ref[...]).astype(BF16)
        f_ref[...] = jnp.dot(hn_ref[:, 0:d], wf_ref[...], preferred_element_type=F32)

    acc = jnp.dot(hn_ref[:, 0:d], w_ref[...], preferred_element_type=F32)
    for c in range(acc.shape[1] // LANES):
        qkv_ref[c] = acc[:, c * LANES:(c + 1) * LANES].astype(BF16)


def _inproj(x, g, w, wf, layer, bm=1024, bn=1024):
    m, d = x.shape
    n = w.shape[2]
    return pl.pallas_call(
        _inproj_kernel,
        grid=(m // bm, n // bn),
        in_specs=[
            pl.BlockSpec((bm, d), lambda i, j: (i, 0)),
            pl.BlockSpec((1, d), lambda i, j: (0, 0)),
            pl.BlockSpec((None, d, bn), lambda i, j: (layer, 0, j)),
            pl.BlockSpec((d, LANES), lambda i, j: (0, 0)),
        ],
        out_specs=[
            pl.BlockSpec((bn // LANES, bm, LANES), lambda i, j: (j, i, 0)),
            pl.BlockSpec((bm, LANES), lambda i, j: (i, 0)),
        ],
        out_shape=[
            jax.ShapeDtypeStruct((n // LANES, m, LANES), BF16),
            jax.ShapeDtypeStruct((m, LANES), F32),
        ],
        scratch_shapes=[pltpu.VMEM((bm, d + LHS_PAD), BF16)],
        compiler_params=_params("arbitrary", "arbitrary"),
        name="inproj",
    )(x, g, w, wf)


def _gate_cumsum_kernel(f_ref, b_ref, ccol_ref, crow_ref):
    t = f_ref.shape[0]
    blk = LANES
    r_i = lax.broadcasted_iota(jnp.int32, (blk, blk), 0)
    c_i = lax.broadcasted_iota(jnp.int32, (blk, blk), 1)
    tri = jnp.where(c_i <= r_i, 1.0, 0.0).astype(BF16)
    carry = jnp.zeros((1, LANES), F32)
    for n in range(t // blk):
        lf = _log_sigmoid(f_ref[n * blk:(n + 1) * blk, :] + b_ref[...])
        h1 = lf.astype(BF16)
        r1 = lf - h1.astype(F32)
        h2 = r1.astype(BF16)
        h3 = (r1 - h2.astype(F32)).astype(BF16)
        c = (jnp.dot(tri, h1, preferred_element_type=F32)
             + jnp.dot(tri, h2, preferred_element_type=F32)
             + jnp.dot(tri, h3, preferred_element_type=F32)) + carry
        ccol_ref[n * blk:(n + 1) * blk, :] = c
        crow_ref[:, n * blk:(n + 1) * blk] = jnp.transpose(c)[0:SUBLANES, :]
        carry = c[blk - 1:blk, :]


def _gate_cumsum(f, b_pad, t):
    m = f.shape[0]
    nb = m // t
    return pl.pallas_call(
        _gate_cumsum_kernel,
        grid=(nb,),
        in_specs=[
            pl.BlockSpec((t, LANES), lambda b: (b, 0)),
            pl.BlockSpec((1, LANES), lambda b: (0, 0)),
        ],
        out_specs=[
            pl.BlockSpec((t, LANES), lambda b: (b, 0)),
            pl.BlockSpec((None, SUBLANES, t), lambda b: (b, 0, 0)),
        ],
        out_shape=[
            jax.ShapeDtypeStruct((m, LANES), F32),
            jax.ShapeDtypeStruct((nb, SUBLANES, t), F32),
        ],
        compiler_params=_params("arbitrary"),
        name="gate_cumsum",
    )(f, b_pad)


def _dilated_kernel(slopes_ref, q_ref, k_ref, v_ref, g_ref, o_ref,
                    qf_ref, kf_ref, vf_ref, qg_ref, kg_ref, vg_ref, qd_ref, kd_ref, va_ref, s_ref, p_ref,
                    mx_ref, oacc_ref, lse_ref):
    h = pl.program_id(1)
    t = q_ref.shape[0]
    wb = WINDOW_BLOCK
    slope = slopes_ref[h]
    qf_ref[...] = q_ref[...].astype(F32)
    kf_ref[...] = k_ref[...].astype(F32)
    vf_ref[...] = v_ref[...].astype(F32)
    for p in range(len(DILATION_PATTERNS)):
        kd_ref[p, 0:wb, :] = jnp.zeros((wb, HEAD_DIM), BF16)
        va_ref[p, 0:wb, :] = jnp.zeros((wb, 2 * HEAD_DIM), BF16)
        va_ref[p, wb:, HEAD_DIM:] = jnp.ones((t, HEAD_DIM), BF16)

    q_i = lax.broadcasted_iota(jnp.int32, (wb, 2 * wb), 0)
    k_i = lax.broadcasted_iota(jnp.int32, (wb, 2 * wb), 1)
    delta_i = wb + q_i - k_i
    delta = delta_i.astype(F32)
    in_window = (delta_i >= 0) & (delta_i <= wb)
    rc = 256
    prev_dil, (qsrc, ksrc, vsrc) = 1, (qf_ref, kf_ref, vf_ref)

    for p, (window, dil) in enumerate(DILATION_PATTERNS):
        assert window // dil == wb and t % (dil * wb) == 0
        seq = t // dil
        nblk = seq // wb
        has_prev = nblk > 1
        bias_band = jnp.where(in_window, (-slope * dil) * delta, NEG)
        bias_first = jnp.where(k_i >= wb, bias_band, NEG)
        if dil == 1:
            qd_ref[p] = q_ref[...]
            kd_ref[p, wb:, :] = k_ref[...]
            va_ref[p, wb:, 0:HEAD_DIM] = v_ref[...]
        else:
            step = dil // prev_dil
            assert prev_dil * step == dil
            keep = p + 1 < len(DILATION_PATTERNS)
            for src_ref, keep_ref, dst_ref, row0, lanes in (
                    (qsrc, qg_ref, qd_ref, 0, slice(None)),
                    (ksrc, kg_ref, kd_ref, wb, slice(None)),
                    (vsrc, vg_ref, va_ref, wb, slice(0, HEAD_DIM))):
                for r1 in range(prev_dil):
                    for r2 in range(step):
                        r = r1 + prev_dil * r2
                        x = src_ref[pl.ds(r1 * (t // prev_dil) + r2, seq, stride=step), :]
                        if keep:
                            keep_ref[r * seq:(r + 1) * seq, :] = x
                        dst_ref[p, row0 + r * seq:row0 + (r + 1) * seq, lanes] = x.astype(BF16)
            if keep:
                qsrc, ksrc, vsrc = qg_ref, kg_ref, vg_ref
        prev_dil = dil
        width = 2 * wb if has_prev else wb
        for blk in range(t // wb):
            rows = slice(blk * wb, (blk + 1) * wb)
            if has_prev:
                band = kd_ref[p, blk * wb:(blk + 2) * wb, :]
                bias = bias_first if blk % nblk == 0 else bias_band
            else:
                band = kd_ref[p, (blk + 1) * wb:(blk + 2) * wb, :]
                bias = bias_band[:, wb:]
            s_ref[rows, 0:width] = _nt_dot(qd_ref[p, rows, :], band) + bias
        for c in range(t // rc):
            rows = slice(c * rc, (c + 1) * rc)
            s = s_ref[rows, 0:width]
            mx = jnp.max(s, axis=-1, keepdims=True)
            p_ref[p, rows, 0:width] = jnp.exp(s - mx).astype(BF16)
            mx_ref[rows, :] = jnp.broadcast_to(mx, (rc, LANES))
        for blk in range(t // wb):
            rows = slice(blk * wb, (blk + 1) * wb)
            vrows = slice(blk * wb, (blk + 2) * wb) if has_prev else slice((blk + 1) * wb, (blk + 2) * wb)
            o2 = jnp.dot(p_ref[p, rows, 0:width], va_ref[p, vrows, :], preferred_element_type=F32)
            l = o2[:, HEAD_DIM:]
            r, n = divmod(blk, nblk)
            if dil == 16:
                r1, r2 = r % 4, r // 4
                dest = pl.ds(r1 * (t // 4) + r2, wb, stride=4)
                qg_ref[dest, :] = o2[:, 0:HEAD_DIM] / l
                kg_ref[dest, :] = mx_ref[rows, :] + jnp.log(l)
            else:
                dest = pl.ds(r + dil * wb * n, wb, stride=dil) if dil > 1 else pl.ds(blk * wb, wb)
                oacc_ref[p, dest, :] = o2[:, 0:HEAD_DIM] / l
                lse_ref[p, dest, :] = mx_ref[rows, :] + jnp.log(l)
        if dil == 16:
            for r1 in range(4):
                seg = slice(r1 * (t // 4), (r1 + 1) * (t // 4))
                oacc_ref[p, pl.ds(r1, t // 4, stride=4), :] = qg_ref[seg, :]
                lse_ref[p, pl.ds(r1, t // 4, stride=4), :] = kg_ref[seg, :]

    g = g_ref[pl.ds(h, 1), :]
    for c in range(t // rc):
        rs = slice(c * rc, (c + 1) * rc)
        l0, l1, l2 = lse_ref[0, rs, :], lse_ref[1, rs, :], lse_ref[2, rs, :]
        mx = jnp.maximum(jnp.maximum(l0, l1), l2)
        e0, e1, e2 = jnp.exp(l0 - mx), jnp.exp(l1 - mx), jnp.exp(l2 - mx)
        o = (e0 * oacc_ref[0, rs, :] + e1 * oacc_ref[1, rs, :] + e2 * oacc_ref[2, rs, :]) / (e0 + e1 + e2)
        o_ref[rs, :] = _head_rmsnorm(o, g).astype(BF16)


def _dilated_attention(qkv, g_head, slopes, nb, t):
    m = nb * t
    npat = len(DILATION_PATTERNS)
    wb = WINDOW_BLOCK
    grid_spec = pltpu.PrefetchScalarGridSpec(
        num_scalar_prefetch=1,
        grid=(nb, N_HEADS_A),
        in_specs=[
            pl.BlockSpec((None, t, HEAD_DIM), lambda b, h, s: (QA0 + h, b, 0)),
            pl.BlockSpec((None, t, HEAD_DIM), lambda b, h, s: (KA0 + h, b, 0)),
            pl.BlockSpec((None, t, HEAD_DIM), lambda b, h, s: (VA0 + h, b, 0)),
            pl.BlockSpec((N_HEADS, HEAD_DIM), lambda b, h, s: (0, 0)),
        ],
        out_specs=pl.BlockSpec((t, HEAD_DIM), lambda b, h, s: (b, h)),
        scratch_shapes=[
            pltpu.VMEM((t, HEAD_DIM), F32),
            pltpu.VMEM((t, HEAD_DIM), F32),
            pltpu.VMEM((t, HEAD_DIM), F32),
            pltpu.VMEM((t, HEAD_DIM), F32),
            pltpu.VMEM((t, HEAD_DIM), F32),
            pltpu.VMEM((t, HEAD_DIM), F32),
            pltpu.VMEM((npat, t, HEAD_DIM), BF16),
            pltpu.VMEM((npat, t + wb, HEAD_DIM), BF16),
            pltpu.VMEM((npat, t + wb, 2 * HEAD_DIM), BF16),
            pltpu.VMEM((t, 2 * wb), F32),
            pltpu.VMEM((npat, t, 2 * wb), BF16),
            pltpu.VMEM((t, LANES), F32),
            pltpu.VMEM((npat, t, HEAD_DIM), F32),
            pltpu.VMEM((npat, t, LANES), F32),
        ],
    )
    return pl.pallas_call(
        _dilated_kernel,
        grid_spec=grid_spec,
        out_shape=jax.ShapeDtypeStruct((m, N_HEADS_A * HEAD_DIM), BF16),
        compiler_params=_params("arbitrary", "arbitrary"),
        name="dilated_attention",
    )(slopes, qkv, qkv, qkv, g_head)


def _stickbreak_kernel(q_ref, k_ref, v_ref, g_ref, o_ref, a_ref, lsig_ref, l1m_ref):
    j = pl.program_id(1)
    t = q_ref.shape[0]
    bq, ck = ATT_BQ, ATT_CHUNK
    assert bq == ck
    g = g_ref[pl.ds(N_HEADS_A + j, 1), :]
    r_i = lax.broadcasted_iota(jnp.int32, (ck, ck), 0)
    c_i = lax.broadcasted_iota(jnp.int32, (ck, ck), 1)
    suffix = jnp.where(r_i > c_i, 1.0, 0.0).astype(BF16)
    causal = c_i < r_i
    for n in range(t // bq):
        q = q_ref[n * bq:(n + 1) * bq, :]
        pair0 = n * (n + 1) // 2
        row_sum = {}
        for c in range(n + 1):
            z2 = _nt_dot(q, k_ref[c * ck:(c + 1) * ck, :])
            log_1m = jnp.minimum(-z2, 0.0) - jnp.log2(1.0 + jnp.exp2(-jnp.abs(z2)))
            lsig_ref[:, c * ck:(c + 1) * ck] = z2 + log_1m
            if c == n:
                log_1m = jnp.where(causal, log_1m, 0.0)
            l1m_ref[pair0 + c] = log_1m.astype(BF16)
            if c > 0:
                row_sum[c] = jnp.sum(log_1m, axis=-1, keepdims=True)
        later = jnp.zeros((bq, 1), F32)
        for c in range(n, -1, -1):
            after = jnp.dot(l1m_ref[pair0 + c], suffix, preferred_element_type=F32) + later
            a = jnp.exp2(lsig_ref[:, c * ck:(c + 1) * ck] + after)
            if c == n:
                a = jnp.where(causal, a, 0.0)
            a_ref[n, :, c * ck:(c + 1) * ck] = a.astype(BF16)
            if c > 0:
                later = later + row_sum[c]
        ext = (n + 1) * bq
        o = jnp.dot(a_ref[n, :, 0:ext], v_ref[0:ext, :], preferred_element_type=F32)
        o_ref[n * bq:(n + 1) * bq, :] = _head_rmsnorm(o, g).astype(BF16)


def _stickbreak_attention(qkv, g_head, nb, t):
    m = nb * t
    return pl.pallas_call(
        _stickbreak_kernel,
        grid=(nb, N_HEADS_B),
        in_specs=[
            pl.BlockSpec((None, t, HEAD_DIM), lambda b, j: (QB0 + j, b, 0)),
            pl.BlockSpec((None, t, HEAD_DIM), lambda b, j: (KB0 + j, b, 0)),
            pl.BlockSpec((None, t, HEAD_DIM), lambda b, j: (VB0 + j, b, 0)),
            pl.BlockSpec((N_HEADS, HEAD_DIM), lambda b, j: (0, 0)),
        ],
        out_specs=pl.BlockSpec((t, HEAD_DIM), lambda b, j: (b, j)),
        out_shape=jax.ShapeDtypeStruct((m, N_HEADS_B * HEAD_DIM), BF16),
        scratch_shapes=[
            pltpu.VMEM((t // ATT_BQ, ATT_BQ, t), BF16),
            pltpu.VMEM((ATT_BQ, t), F32),
            pltpu.VMEM(((t // ATT_BQ) * (t // ATT_BQ + 1) // 2, ATT_BQ, ATT_CHUNK), BF16),
        ],
        compiler_params=_params("arbitrary", "arbitrary"),
        name="stickbreak_attention",
    )(qkv, qkv, qkv, g_head)


def _forgetting_kernel(q_ref, k_ref, v_ref, ccol_ref, crow_ref, g_ref, o_ref, s_ref, p_ref):
    j = pl.program_id(1)
    t = q_ref.shape[0]
    bq, ck = ATT_BQ, FOX_CHUNK
    g = g_ref[pl.ds(N_HEADS_A + N_HEADS_B + j, 1), :]
    lane = lax.broadcasted_iota(jnp.int32, (1, LANES), 1)
    r_i = lax.broadcasted_iota(jnp.int32, (bq, bq), 0)
    c_i = lax.broadcasted_iota(jnp.int32, (bq, bq), 1)
    causal = c_i <= r_i
    for n in range(t // bq):
        q0, ext = n * bq, (n + 1) * bq
        q = q_ref[q0:ext, :]
        c_t = jnp.sum(jnp.where(lane == j, ccol_ref[q0:ext, :], 0.0), axis=-1, keepdims=True)
        mx = None
        for k0 in range(0, ext, ck):
            k1 = min(k0 + ck, ext)
            s = _nt_dot(q, k_ref[k0:k1, :]) + c_t - crow_ref[pl.ds(j, 1), k0:k1]
            if k1 == ext:
                split = k1 - k0 - bq
                diag = jnp.where(causal, s[:, split:], NEG)
                s = diag if split == 0 else jnp.concatenate([s[:, :split], diag], axis=1)
            s_ref[:, k0:k1] = s
            cm = jnp.max(s, axis=-1, keepdims=True)
            mx = cm if mx is None else jnp.maximum(mx, cm)
        l = jnp.zeros((bq, 1), F32)
        for k0 in range(0, ext, ck):
            k1 = min(k0 + ck, ext)
            p = jnp.exp(s_ref[:, k0:k1] - mx)
            l = l + jnp.sum(p, axis=-1, keepdims=True)
            p_ref[n, :, k0:k1] = p.astype(BF16)
        o = jnp.dot(p_ref[n, :, 0:ext], v_ref[0:ext, :], preferred_element_type=F32) / l
        o_ref[q0:ext, :] = _head_rmsnorm(o, g).astype(BF16)


def _forgetting_attention(qkv, ccol, crow, g_head, nb, t):
    m = nb * t
    return pl.pallas_call(
        _forgetting_kernel,
        grid=(nb, N_HEADS_C),
        in_specs=[
            pl.BlockSpec((None, t, HEAD_DIM), lambda b, j: (QC0 + j, b, 0)),
            pl.BlockSpec((None, t, HEAD_DIM), lambda b, j: (KC0 + j, b, 0)),
            pl.BlockSpec((None, t, HEAD_DIM), lambda b, j: (VC0 + j, b, 0)),
            pl.BlockSpec((t, LANES), lambda b, j: (b, 0)),
            pl.BlockSpec((None, SUBLANES, t), lambda b, j: (b, 0, 0)),
            pl.BlockSpec((N_HEADS, HEAD_DIM), lambda b, j: (0, 0)),
        ],
        out_specs=pl.BlockSpec((t, HEAD_DIM), lambda b, j: (b, j)),
        out_shape=jax.ShapeDtypeStruct((m, N_HEADS_C * HEAD_DIM), BF16),
        scratch_shapes=[pltpu.VMEM((ATT_BQ, t), F32), pltpu.VMEM((t // ATT_BQ, ATT_BQ, t), BF16)],
        compiler_params=_params("arbitrary", "arbitrary"),
        name="forgetting_attention",
    )(qkv, qkv, qkv, ccol, crow, g_head)


def _outproj_kernel(x_ref, oa_ref, ob_ref, oc_ref, w_ref, g_ref, x1_ref, hn_ref):
    ka = oa_ref.shape[1]
    kb = ka + ob_ref.shape[1]
    x1 = (x_ref[...]
          + jnp.dot(oa_ref[...], w_ref[0:ka, :], preferred_element_type=F32)
          + jnp.dot(ob_ref[...], w_ref[ka:kb, :], preferred_element_type=F32)
          + jnp.dot(oc_ref[...], w_ref[kb:, :], preferred_element_type=F32))
    x1_ref[...] = x1
    d = x1.shape[1]
    y = x1 * lax.rsqrt(jnp.mean(x1 * x1, axis=-1, keepdims=True) + EPS)
    hn_ref[:, 0:d] = (y * g_ref[...]).astype(BF16)
    hn_ref[:, d:] = jnp.zeros((x1.shape[0], hn_ref.shape[1] - d), BF16)


def _outproj(x, oa, ob, oc, w, g, layer, bm=512):
    m, d = x.shape
    row = lambda i: (i, 0)
    whole = lambda i: (0, 0)
    return pl.pallas_call(
        _outproj_kernel,
        grid=(m // bm,),
        in_specs=[
            pl.BlockSpec((bm, d), row),
            pl.BlockSpec((bm, oa.shape[1]), row),
            pl.BlockSpec((bm, ob.shape[1]), row),
            pl.BlockSpec((bm, oc.shape[1]), row),
            pl.BlockSpec((None, d, d), lambda i: (layer, 0, 0)),
            pl.BlockSpec((1, d), whole),
        ],
        out_specs=[pl.BlockSpec((bm, d), row), pl.BlockSpec((bm, d + LHS_PAD), row)],
        out_shape=[jax.ShapeDtypeStruct((m, d), F32), jax.ShapeDtypeStruct((m, d + LHS_PAD), BF16)],
        compiler_params=_params("arbitrary"),
        name="outproj",
    )(x, oa, ob, oc, w, g)


def _ffn_up_kernel(hn_ref, wg_ref, wu_ref, cwg_ref, cwu_ref, cbg_ref, cbu_ref, h_ref,
                   w16_ref, halo_ref, raw0_ref, raw1_ref, stage_ref, *, ni, tiles_per_seq, cw, mp, ep):
    s = pl.program_id(0)
    last = pl.num_programs(0) - 2
    i = lax.rem(jnp.minimum(s, last), ni)
    ip = lax.rem(jnp.maximum(s - 1, 0), ni)
    bm, (d, bf) = hn_ref.shape[0], wg_ref.shape
    pad = SUBLANES
    spc = cw // LANES
    half = ep // 2

    @pl.when(i == 0)
    def _():
        for c in range(bf // cw):
            w16_ref[:, 2 * c * cw:(2 * c + 1) * cw] = wg_ref[:, c * cw:(c + 1) * cw].astype(BF16)
            w16_ref[:, (2 * c + 1) * cw:(2 * c + 2) * cw] = wu_ref[:, c * cw:(c + 1) * cw].astype(BF16)

    @pl.when(s == 0)
    def _():
        raw1_ref[...] = jnp.zeros(raw1_ref.shape, F32)

    @pl.when(lax.rem(ip, tiles_per_seq) == 0)
    def _():
        halo_ref[...] = jnp.zeros(halo_ref.shape, F32)

    def step(cur_ref, prev_ref):
        prev_ref[:, 0:pad, :] = halo_ref[...]
        halo_ref[...] = prev_ref[:, bm:bm + pad, :]
        for c in range(bf // cw):
            for m0 in range(0, bm, mp):
                res = jnp.dot(hn_ref[m0:m0 + mp, 0:d], w16_ref[:, 2 * c * cw:(2 * c + 2) * cw],
                              preferred_element_type=F32)
                for k in range(2 * spc):
                    cur_ref[2 * c * spc + k, pad + m0:pad + m0 + mp, :] = res[:, k * LANES:(k + 1) * LANES]
                for k in range(spc):
                    g_slab, u_slab = 2 * c * spc + k, (2 * c + 1) * spc + k
                    cols = slice(c * cw + k * LANES, c * cw + (k + 1) * LANES)
                    for r0 in range(m0, m0 + mp, ep):

                        def conv(slab, cw_ref, cb_ref):
                            def rows(first):
                                return prev_ref[slab, pl.ds(pad + r0 + first, half, stride=2), :]
                            w0, w1, w2 = cw_ref[0:1, cols], cw_ref[1:2, cols], cw_ref[2:3, cols]
                            b = cb_ref[:, cols]
                            x_e, x_o, x_em, x_om = rows(0), rows(1), rows(-2), rows(-1)
                            return (b + w0 * x_em + w1 * x_om + w2 * x_e,
                                    b + w0 * x_om + w1 * x_e + w2 * x_o)

                        gate_e, gate_o = conv(g_slab, cwg_ref, cbg_ref)
                        up_e, up_o = conv(u_slab, cwu_ref, cbu_ref)
                        stage_ref[pl.ds(r0, half, stride=2), :] = (
                            gate_e * (1.0 / (1.0 + jnp.exp(-gate_e))) * up_e)
                        stage_ref[pl.ds(r0 + 1, half, stride=2), :] = (
                            gate_o * (1.0 / (1.0 + jnp.exp(-gate_o))) * up_o)
                        h_ref[r0:r0 + ep, cols] = stage_ref[r0:r0 + ep, :].astype(BF16)

    @pl.when(lax.rem(s, 2) == 0)
    def _():
        step(raw0_ref, raw1_ref)

    @pl.when(lax.rem(s, 2) == 1)
    def _():
        step(raw1_ref, raw0_ref)


def _ffn_up(hn, w_gu, conv_w, conv_b, t, layer, bm=1024, bf=512, cw=256, mp=256, ep=128):
    m, dp = hn.shape
    d = w_gu.shape[1]
    dff = w_gu.shape[2] // 2
    ni, nj = m // bm, dff // bf
    n = ni * nj
    assert t % bm == 0 and bf % cw == 0

    def mm_tile(s):
        q = jnp.minimum(s, n - 1)
        return lax.rem(q, ni), lax.div(q, ni)

    def ep_tile(s):
        q = jnp.maximum(s - 1, 0)
        return lax.rem(q, ni), lax.div(q, ni)

    nslab = 2 * bf // LANES
    raw = pltpu.VMEM((nslab, bm + SUBLANES, LANES), F32)
    return pl.pallas_call(
        functools.partial(_ffn_up_kernel, ni=ni, tiles_per_seq=t // bm, cw=cw, mp=mp, ep=ep),
        grid=(n + 1,),
        in_specs=[
            pl.BlockSpec((bm, dp), lambda s: (mm_tile(s)[0], 0)),
            pl.BlockSpec((None, d, bf), lambda s: (layer, 0, mm_tile(s)[1])),
            pl.BlockSpec((None, d, bf), lambda s: (layer, 0, nj + mm_tile(s)[1])),
            pl.BlockSpec((None, CONV_WIDTH, bf), lambda s: (layer, 0, ep_tile(s)[1])),
            pl.BlockSpec((None, CONV_WIDTH, bf), lambda s: (layer, 0, nj + ep_tile(s)[1])),
            pl.BlockSpec((None, 1, bf), lambda s: (layer, 0, ep_tile(s)[1])),
            pl.BlockSpec((None, 1, bf), lambda s: (layer, 0, nj + ep_tile(s)[1])),
        ],
        out_specs=pl.BlockSpec((bm, bf), lambda s: ep_tile(s)),
        out_shape=jax.ShapeDtypeStruct((m, dff), BF16),
        scratch_shapes=[
            pltpu.VMEM((d, 2 * bf), BF16),
            pltpu.VMEM((nslab, SUBLANES, LANES), F32),
            raw, raw,
            pltpu.VMEM((bm, LANES), F32),
        ],
        compiler_params=_params("arbitrary"),
        name="ffn_up",
    )(hn, w_gu, w_gu, conv_w, conv_w, conv_b, conv_b)


def _ffn_down_kernel(x_ref, h_ref, w_ref, o_ref):
    o_ref[...] = x_ref[...] + jnp.dot(h_ref[...], w_ref[...], preferred_element_type=F32)


def _ffn_down(x, h, w, layer, bm=1024, bn=512):
    m, d = x.shape
    k = h.shape[1]
    return pl.pallas_call(
        _ffn_down_kernel,
        grid=(m // bm, d // bn),
        in_specs=[
            pl.BlockSpec((bm, bn), lambda i, j: (i, j)),
            pl.BlockSpec((bm, k), lambda i, j: (i, 0)),
            pl.BlockSpec((None, k, bn), lambda i, j: (layer, 0, j)),
        ],
        out_specs=pl.BlockSpec((bm, bn), lambda i, j: (i, j)),
        out_shape=jax.ShapeDtypeStruct((m, d), F32),
        compiler_params=_params("arbitrary", "arbitrary"),
        name="ffn_down",
    )(x, h, w)


def _rmsnorm_kernel(x_ref, g_ref, o_ref):
    x = x_ref[...]
    y = x * lax.rsqrt(jnp.mean(x * x, axis=-1, keepdims=True) + EPS)
    o_ref[...] = y * g_ref[...]


def _rmsnorm(x, g, bm=1024):
    m, d = x.shape
    return pl.pallas_call(
        _rmsnorm_kernel,
        grid=(m // bm,),
        in_specs=[pl.BlockSpec((bm, d), lambda i: (i, 0)), pl.BlockSpec((1, d), lambda i: (0, 0))],
        out_specs=pl.BlockSpec((bm, d), lambda i: (i, 0)),
        out_shape=jax.ShapeDtypeStruct((m, d), F32),
        compiler_params=_params("arbitrary"),
        name="final_rmsnorm",
    )(x, g)


def _layer(x, nb, t, layer, g_mix, w_qkv, w_f, b_f, g_head, w_o, g_ffn, w_gu, conv_w, conv_b, w_down,
           slopes):
    d = x.shape[1]
    b_pad = jnp.pad(b_f, (0, LANES - N_HEADS_C)).reshape(1, LANES)
    qkv, f = _inproj(x, g_mix.reshape(1, d), w_qkv, w_f, layer)
    ccol, crow = _gate_cumsum(f, b_pad, t)
    o_a = _dilated_attention(qkv, g_head, slopes, nb, t)
    o_b = _stickbreak_attention(qkv, g_head, nb, t)
    o_c = _forgetting_attention(qkv, ccol, crow, g_head, nb, t)
    x1, hn = _outproj(x, o_a, o_b, o_c, w_o, g_ffn.reshape(1, d), layer)
    h = _ffn_up(hn, w_gu, conv_w, conv_b, t, layer)
    return _ffn_down(x1, h, w_down, layer)


def _query_column_scale():
    scale = 1.0 / math.sqrt(HEAD_DIM)
    blk = jnp.arange(N_QKV_BLOCKS)
    is_qb = (blk >= QB0) & (blk < KB0)
    is_q = (blk < KA0) | is_qb | ((blk >= QC0) & (blk < KC0))
    per_block = jnp.where(is_qb, scale * LOG2_E, jnp.where(is_q, scale, 1.0)).astype(F32)
    return jnp.repeat(per_block, HEAD_DIM).reshape(1, -1)


def kernel(x, g_mix, w_in, b_f, g_head, w_o, g_ffn, w_gu, conv_w, conv_b, w_down, g_final):
    nb, t, d = x.shape
    depth = g_mix.shape[0]
    n_qkv = N_QKV_BLOCKS * HEAD_DIM
    slopes = 2.0 ** (-8.0 * (jnp.arange(N_HEADS_A, dtype=F32) + 1.0) / N_HEADS_A)
    w_qkv = _cast_weights(w_in, _query_column_scale(), bk=256)
    w_f = jnp.pad(w_in[:, :, n_qkv:], ((0, 0), (0, 0), (0, LANES - N_HEADS_C))).astype(BF16)
    w_o16 = _cast_weights(w_o, jnp.ones((1, d), F32), bk=1024)
    w_down16 = _cast_weights(w_down, jnp.ones((1, d), F32), bk=w_down.shape[1] // 8)
    conv_b3 = conv_b.reshape(depth, 1, -1)
    xf = x.reshape(nb * t, d)
    for layer in range(depth):
        xf = _layer(xf, nb, t, layer, g_mix[layer], w_qkv, w_f[layer], b_f[layer], g_head[layer], w_o16,
                    g_ffn[layer], w_gu, conv_w, conv_b3, w_down16, slopes)
    return _rmsnorm(xf, g_final.reshape(1, d)).reshape(nb, t, d)
```
